```python
import math
import jax, jax.numpy as jnp
from jax import lax
import numpy as np

D_MODEL = 1024
BATCH = 2
SEQ = 8192
DEPTH = 4
DEC_BATCH = 128
DEC_SEQ = 8
PAST_LEN = 2048
PAGE_SIZE = 128

CONV_WIDTH = 31
GROUPS = ((128, 1), (512, 4), (2048, 16))
N_GROUPS = len(GROUPS)
HEADS_PER_GROUP = 8
HEAD_DIM = 64
ATTN_WIDTH = HEADS_PER_GROUP * HEAD_DIM
ROPE_THETA = 10000.0
D_FF = 3584
D_FF_EXPERT = 3584
N_EXPERTS = 8
TOP_K = 2
ALPHA = (2 * DEPTH) ** 0.25
BETA = (8 * DEPTH) ** -0.25
LN_EPS = 1e-5
N_CONV_LAYERS = (DEPTH + 1) // 2
N_ATTN_LAYERS = DEPTH // 2
NEG_INF = -1e30

kernel_name = "conformer_conv_dilated_swa_hybrid_step"


def layer_norm(x, g, b):
    xf = x.astype(jnp.float32)
    mu = jnp.mean(xf, axis=-1, keepdims=True)
    var = jnp.mean(jnp.square(xf - mu), axis=-1, keepdims=True)
    y = (xf - mu) * lax.rsqrt(var + LN_EPS) * g.astype(jnp.float32) + b.astype(jnp.float32)
    return y.astype(x.dtype)


def rope(x, pos):
    half = HEAD_DIM // 2
    inv = ROPE_THETA ** (-jnp.arange(half, dtype=jnp.float32) / half)
    ang = pos.astype(jnp.float32)[:, None] * inv[None, :]
    cos = jnp.cos(ang)[None, :, None, :].astype(x.dtype)
    sin = jnp.sin(ang)[None, :, None, :].astype(x.dtype)
    x1, x2 = x[..., :half], x[..., half:]
    return jnp.concatenate([x1 * cos - x2 * sin, x2 * cos + x1 * sin], axis=-1)


def conv_module(x, prev, w_pw1, b_pw1, w_dw, b_dw, cn_g, cn_b, w_pw2, b_pw2):
    a, gate = jnp.split(x @ w_pw1 + b_pw1, 2, axis=-1)
    u = a * jax.nn.sigmoid(gate)
    up = jnp.concatenate([prev.astype(u.dtype), u], axis=1)
    y = lax.conv_general_dilated(up, w_dw[:, None, :], (1,), 'VALID',
                                 dimension_numbers=('NWC', 'WIO', 'NWC'),
                                 feature_group_count=D_MODEL) + b_dw
    y = layer_norm(y, cn_g, cn_b)
    y = y * jax.nn.sigmoid(y)
    return y @ w_pw2 + b_pw2, up[:, -(CONV_WIDTH - 1):]


def attn_qkv(x, w_qkv, pos):
    B, T, _ = x.shape
    qkv = (x @ w_qkv).reshape(B, T, 3, N_GROUPS * HEADS_PER_GROUP, HEAD_DIM)
    q = rope(qkv[:, :, 0], pos)
    k = rope(qkv[:, :, 1], pos)
    return q, k, qkv[:, :, 2]


def dilated_band_attention(q, k, v, window, dil):
    B, T, H, hd = q.shape
    nw = window // dil
    nb = -(-T // window)
    tpad = nb * window
    m = tpad // dil

    def blocks(a):
        a = jnp.pad(a, ((0, 0), (0, tpad - T), (0, 0), (0, 0)))
        a = a.reshape(B, m, dil, H, hd).transpose(0, 2, 1, 3, 4)
        return a.reshape(B, dil, nb, nw, H, hd)

    def with_prev(a):
        prev = jnp.pad(a[:, :, :-1], ((0, 0), (0, 0), (1, 0), (0, 0), (0, 0), (0, 0)))
        return jnp.concatenate([prev, a], axis=3)

    qb = blocks(q)
    kk, vv = with_prev(blocks(k)), with_prev(blocks(v))
    s = jnp.einsum('bcnqhd,bcnkhd->bcnhqk', qb, kk,
                   preferred_element_type=jnp.float32) * (hd ** -0.5)
    qi = jnp.arange(nw)[:, None] + nw
    ki = jnp.arange(2 * nw)[None, :]
    dist = qi - ki
    band = (dist >= 0) & (dist <= nw)
    mask = band[None] & ((jnp.arange(nb)[:, None, None] > 0) | (ki[None] >= nw))
    s = jnp.where(mask[None, None, :, None], s, NEG_INF)
    lse = jax.nn.logsumexp(s, axis=-1)
    p = jnp.exp(s - lse[..., None]).astype(v.dtype)
    o = jnp.einsum('bcnhqk,bcnkhd->bcnqhd', p, vv)
    o = o.reshape(B, dil, m, H, hd).transpose(0, 2, 1, 3, 4).reshape(B, tpad, H, hd)[:, :T]
    lse = lse.transpose(0, 1, 2, 4, 3).reshape(B, dil, m, H).transpose(0, 2, 1, 3)
    lse = lse.reshape(B, tpad, H)[:, :T]
    return o, lse


def dilated_gather_attention(q, kc, vc, window, dil, past_rows):
    Bq, S, H, hd = q.shape
    n = window // dil + 1
    idx = past_rows + jnp.arange(S)[:, None] - dil * jnp.arange(n)[None, :]
    valid = idx >= 0
    flat = jnp.maximum(idx, 0).reshape(-1)
    kg = jnp.take(kc, flat, axis=1).reshape(Bq, S, n, H, hd)
    vg = jnp.take(vc, flat, axis=1).reshape(Bq, S, n, H, hd)
    s = jnp.einsum('bshd,bsjhd->bshj', q, kg,
                   preferred_element_type=jnp.float32) * (hd ** -0.5)
    s = jnp.where(valid[None, :, None, :], s, NEG_INF)
    lse = jax.nn.logsumexp(s, axis=-1)
    p = jnp.exp(s - lse[..., None]).astype(vc.dtype)
    o = jnp.einsum('bshj,bsjhd->bshd', p, vg)
    return o, lse


def merge_groups(outs, lses, w_o):
    o = jnp.stack(outs).astype(jnp.float32)
    wts = jax.nn.softmax(jnp.stack(lses), axis=0)
    y = jnp.sum(wts[..., None] * o, axis=0).astype(outs[0].dtype)
    B, T = y.shape[:2]
    return y.reshape(B, T, ATTN_WIDTH) @ w_o


def attn_prompt(x, w_qkv, w_o):
    T = x.shape[1]
    q, k, v = attn_qkv(x, w_qkv, jnp.arange(T))
    outs, lses, bufs = [], [], []
    for g, (win, dil) in enumerate(GROUPS):
        sl = slice(g * HEADS_PER_GROUP, (g + 1) * HEADS_PER_GROUP)
        o, l = dilated_band_attention(q[:, :, sl], k[:, :, sl], v[:, :, sl], win, dil)
        outs.append(o)
        lses.append(l)
        keep = min(win, T)
        bufs.append((k[:, T - keep:, sl], v[:, T - keep:, sl]))
    return merge_groups(outs, lses, w_o), bufs


def attn_sample(x, caches, w_qkv, w_o):
    S = x.shape[1]
    q, k, v = attn_qkv(x, w_qkv, PAST_LEN + jnp.arange(S))
    outs, lses, bufs = [], [], []
    for g, (win, dil) in enumerate(GROUPS):
        sl = slice(g * HEADS_PER_GROUP, (g + 1) * HEADS_PER_GROUP)
        kb, vb = caches[g]
        L = kb.shape[1]
        kc = jnp.concatenate([kb.astype(k.dtype), k[:, :, sl]], axis=1)
        vc = jnp.concatenate([vb.astype(v.dtype), v[:, :, sl]], axis=1)
        o, l = dilated_gather_attention(q[:, :, sl], kc, vc, win, dil, L)
        outs.append(o)
        lses.append(l)
        keep = min(win, PAST_LEN + S)
        bufs.append((kc[:, -keep:], vc[:, -keep:]))
    return merge_groups(outs, lses, w_o), bufs


def swiglu(x, w_gu, w_down):
    g, u = jnp.split(x @ w_gu, 2, axis=-1)
    return (jax.nn.silu(g) * u) @ w_down


def moe_ffn(x, w_router, b_router, w_exp_gu, w_exp_down):
    shp = x.shape
    xf = x.reshape(-1, D_MODEL)
    logits = (xf @ w_router).astype(jnp.float32) + b_router.astype(jnp.float32)
    topv, topi = lax.top_k(logits, TOP_K)
    gates = jax.nn.softmax(topv, axis=-1)
    combine = jnp.sum(jax.nn.one_hot(topi, N_EXPERTS, dtype=jnp.float32) * gates[..., None], axis=1)
    y = jnp.zeros_like(xf)
    for e in range(N_EXPERTS):
        y = y + combine[:, e:e + 1].astype(xf.dtype) * swiglu(xf, w_exp_gu[e], w_exp_down[e])
    return y.reshape(shp)


def setup_inputs(seed: int = 0) -> dict:
    key = jax.random.key(seed)
    ks = iter(jax.random.split(key, 32))

    def nrm(shape, scale=1.0):
        return scale * jax.random.normal(next(ks), shape, jnp.float32)

    cache_len = [min(w, PAST_LEN) for w, _ in GROUPS]
    cshape = lambda L: (N_ATTN_LAYERS, DEC_BATCH, L, HEADS_PER_GROUP, HEAD_DIM)
    qkv_cols = 3 * N_GROUPS * ATTN_WIDTH
    col_scale = jnp.concatenate([jnp.ones((2 * N_GROUPS * ATTN_WIDTH,), jnp.float32),
                                 jnp.full((N_GROUPS * ATTN_WIDTH,), BETA, jnp.float32)])
    return {
        "x_prompt": nrm((BATCH, SEQ, D_MODEL)),
        "x_sample": nrm((DEC_BATCH, DEC_SEQ, D_MODEL)),
        "state_conv": nrm((N_CONV_LAYERS, DEC_BATCH, CONV_WIDTH - 1, D_MODEL), 0.5),
        "cache_k_w128": nrm(cshape(cache_len[0])),
        "cache_v_w128": nrm(cshape(cache_len[0]), 0.5),
        "cache_k_w512": nrm(cshape(cache_len[1])),
        "cache_v_w512": nrm(cshape(cache_len[1]), 0.5),
        "cache_k_w2048": nrm(cshape(cache_len[2])),
        "cache_v_w2048": nrm(cshape(cache_len[2]), 0.5),
        "ln_g": 1.0 + nrm((DEPTH, 2, D_MODEL), 0.01),
        "ln_b": nrm((DEPTH, 2, D_MODEL), 0.01),
        "w_pw1": nrm((N_CONV_LAYERS, D_MODEL, 2 * D_MODEL), D_MODEL ** -0.5),
        "b_pw1": nrm((N_CONV_LAYERS, 2 * D_MODEL), 0.01),
        "w_dw": nrm((N_CONV_LAYERS, CONV_WIDTH, D_MODEL), CONV_WIDTH ** -0.5),
        "b_dw": nrm((N_CONV_LAYERS, D_MODEL), 0.01),
        "cn_g": 1.0 + nrm((N_CONV_LAYERS, D_MODEL), 0.01),
        "cn_b": nrm((N_CONV_LAYERS, D_MODEL), 0.01),
        "w_pw2": nrm((N_CONV_LAYERS, D_MODEL, D_MODEL), BETA * D_MODEL ** -0.5),
        "b_pw2": nrm((N_CONV_LAYERS, D_MODEL), 0.01),
        "w_qkv": nrm((N_ATTN_LAYERS, D_MODEL, qkv_cols), D_MODEL ** -0.5) * col_scale,
        "w_o": nrm((N_ATTN_LAYERS, ATTN_WIDTH, D_MODEL), BETA * ATTN_WIDTH ** -0.5),
        "w_ffn_gu": nrm((N_CONV_LAYERS, D_MODEL, 2 * D_FF), D_MODEL ** -0.5),
        "w_ffn_down": nrm((N_CONV_LAYERS, D_FF, D_MODEL), BETA * D_FF ** -0.5),
        "w_router": nrm((N_ATTN_LAYERS, D_MODEL, N_EXPERTS), D_MODEL ** -0.5),
        "b_router": nrm((N_ATTN_LAYERS, N_EXPERTS), 0.01),
        "w_exp_gu": nrm((N_ATTN_LAYERS, N_EXPERTS, D_MODEL, 2 * D_FF_EXPERT), D_MODEL ** -0.5),
        "w_exp_down": nrm((N_ATTN_LAYERS, N_EXPERTS, D_FF_EXPERT, D_MODEL), BETA * D_FF_EXPERT ** -0.5),
    }


def reference(x_prompt, x_sample, state_conv, cache_k_w128, cache_v_w128, cache_k_w512,
              cache_v_w512, cache_k_w2048, cache_v_w2048, ln_g, ln_b, w_pw1, b_pw1, w_dw,
              b_dw, cn_g, cn_b, w_pw2, b_pw2, w_qkv, w_o, w_ffn_gu, w_ffn_down, w_router,
              b_router, w_exp_gu, w_exp_down):
    xp, xs = x_prompt, x_sample
    conv_p, conv_s = [], []
    kv_p = [[] for _ in range(2 * N_GROUPS)]
    kv_s = [[] for _ in range(2 * N_GROUPS)]
    for i in range(DEPTH):
        a = i // 2
        if i % 2 == 0:
            prm = (w_pw1[a], b_pw1[a], w_dw[a], b_dw[a], cn_g[a], cn_b[a], w_pw2[a], b_pw2[a])
            zeros_prev = jnp.zeros((xp.shape[0], CONV_WIDTH - 1, D_MODEL), xp.dtype)
            mp, st_p = conv_module(xp, zeros_prev, *prm)
            ms, st_s = conv_module(xs, state_conv[a], *prm)
            conv_p.append(st_p)
            conv_s.append(st_s)
        else:
            caches = ((cache_k_w128[a], cache_v_w128[a]),
                      (cache_k_w512[a], cache_v_w512[a]),
                      (cache_k_w2048[a], cache_v_w2048[a]))
            mp, bufs_p = attn_prompt(xp, w_qkv[a], w_o[a])
            ms, bufs_s = attn_sample(xs, caches, w_qkv[a], w_o[a])
            for g in range(N_GROUPS):
                kv_p[2 * g].append(bufs_p[g][0])
                kv_p[2 * g + 1].append(bufs_p[g][1])
                kv_s[2 * g].append(bufs_s[g][0])
                kv_s[2 * g + 1].append(bufs_s[g][1])
        xp = layer_norm(ALPHA * xp + mp, ln_g[i, 0], ln_b[i, 0])
        xs = layer_norm(ALPHA * xs + ms, ln_g[i, 0], ln_b[i, 0])
        if i % 2 == 0:
            fp = swiglu(xp, w_ffn_gu[a], w_ffn_down[a])
            fs = swiglu(xs, w_ffn_gu[a], w_ffn_down[a])
        else:
            fp = moe_ffn(xp, w_router[a], b_router[a], w_exp_gu[a], w_exp_down[a])
            fs = moe_ffn(xs, w_router[a], b_router[a], w_exp_gu[a], w_exp_down[a])
        xp = layer_norm(ALPHA * xp + fp, ln_g[i, 1], ln_b[i, 1])
        xs = layer_norm(ALPHA * xs + fs, ln_g[i, 1], ln_b[i, 1])
    conv_p_new = jnp.stack(conv_p)
    conv_s_new = jnp.stack(conv_s)
    k128_p, v128_p, k512_p, v512_p, k2048_p, v2048_p = [jnp.stack(t) for t in kv_p]
    k128_s, v128_s, k512_s, v512_s, k2048_s, v2048_s = [jnp.stack(t) for t in kv_s]
    return (xp, xs, conv_p_new, conv_s_new,
            k128_p, v128_p, k512_p, v512_p, k2048_p, v2048_p,
            k128_s, v128_s, k512_s, v512_s, k2048_s, v2048_s)
```

```python
import functools

import jax
import jax.numpy as jnp
from jax import lax
from jax.experimental import pallas as pl
from jax.experimental.pallas import tpu as pltpu

GROUPS = ((128, 1), (512, 4), (2048, 16))
N_GROUPS = len(GROUPS)
HEADS = 8
HEAD_DIM = 64
GROUP_WIDTH = HEADS * HEAD_DIM
BAND = 128
CONV_WIDTH = 31
CONV_HALO = 32
PAST_LEN = 2048
ROPE_THETA = 10000.0
LN_EPS = 1e-5
NEG_INF = -1e30

VMEM_LIMIT_BYTES = 48 * 1024 * 1024
TOKEN_TILE = 1024
FF_TILE = 512

F32 = jnp.float32
BF16 = jnp.bfloat16


def _params(*semantics):
    return pltpu.CompilerParams(dimension_semantics=semantics,
                                vmem_limit_bytes=VMEM_LIMIT_BYTES)


def _layer_norm(x, g, b):
    mu = jnp.mean(x, axis=-1, keepdims=True)
    xc = x - mu
    var = jnp.mean(xc * xc, axis=-1, keepdims=True)
    return xc * lax.rsqrt(var + LN_EPS) * g + b


def _sigmoid(x):
    return 1.0 / (1.0 + jnp.exp(-x))


def _dot(a, b):
    return jnp.dot(a, b, preferred_element_type=F32)


def _dot_nt(a, b):
    return lax.dot_general(a, b, (((1,), (1,)), ((), ())), preferred_element_type=F32)


def _token_tile(t):
    return min(TOKEN_TILE, t)


def _pw1_glu_kernel(x_ref, w_ref, b_ref, u_ref):
    d = u_ref.shape[-1]
    x = x_ref[...].astype(BF16)
    a = _dot(x, w_ref[0, :, :d]) + b_ref[0, :, :d]
    g = _dot(x, w_ref[0, :, d:]) + b_ref[0, :, d:]
    u_ref[...] = a * _sigmoid(g)


def _pw1_glu(x, w_pw1, b_pw1, a):
    t, d = x.shape
    tm = _token_tile(t)
    return pl.pallas_call(
        _pw1_glu_kernel,
        grid=(t // tm,),
        in_specs=[pl.BlockSpec((tm, d), lambda i: (i, 0)),
                  pl.BlockSpec((1, d, 2 * d), lambda i: (a, 0, 0)),
                  pl.BlockSpec((1, 1, 2 * d), lambda i: (a, 0, 0))],
        out_specs=pl.BlockSpec((tm, d), lambda i: (i, 0)),
        out_shape=jax.ShapeDtypeStruct((t, d), F32),
        compiler_params=_params("parallel"),
        name="pw1_glu",
    )(x, w_pw1, b_pw1)


def _conv_tail(y, x, bdw_ref, cng_ref, cnb_ref, w2_ref, b2_ref, lng_ref, lnb_ref, alpha):
    y = y + bdw_ref[0]
    y = _layer_norm(y, cng_ref[0], cnb_ref[0])
    y = y * _sigmoid(y)
    m = _dot(y.astype(BF16), w2_ref[0]) + b2_ref[0]
    return _layer_norm(alpha * x + m, lng_ref[0, 0], lnb_ref[0, 0])


CONV_ROW_CHUNK = 32


def _conv_prompt_kernel(um_ref, uh_ref, x_ref, wdw_ref, bdw_ref, cng_ref, cnb_ref, w2_ref,
                        b2_ref, lng_ref, lnb_ref, o_ref, win_ref, y_ref, *, alpha):
    i = pl.program_id(1)
    tt = um_ref.shape[1]
    win_ref[0:CONV_HALO, :] = jnp.where(i > 0, uh_ref[0], 0.0)
    win_ref[CONV_HALO:CONV_HALO + tt, :] = um_ref[0]
    lead = CONV_HALO - (CONV_WIDTH - 1)
    for r in range(0, tt, CONV_ROW_CHUNK):
        acc = wdw_ref[0, 0:1, :] * win_ref[r + lead:r + lead + CONV_ROW_CHUNK, :]
        for j in range(1, CONV_WIDTH):
            acc = acc + wdw_ref[0, j:j + 1, :] * win_ref[r + lead + j:r + lead + j + CONV_ROW_CHUNK, :]
        y_ref[r:r + CONV_ROW_CHUNK, :] = acc
    o_ref[...] = _conv_tail(y_ref[...], x_ref[...], bdw_ref, cng_ref, cnb_ref, w2_ref, b2_ref,
                            lng_ref, lnb_ref, alpha)


def _conv_prompt(u, x, batch, wts, a, li, alpha):
    w_dw, b_dw, cn_g, cn_b, w_pw2, b_pw2, ln_g, ln_b = wts
    t, d = x.shape
    seq = t // batch
    tt = min(512, seq)
    nt = seq // tt
    hb = tt // CONV_HALO
    u3 = u.reshape(batch, seq, d)
    vec = lambda: pl.BlockSpec((1, 1, d), lambda b, i: (a, 0, 0))
    return pl.pallas_call(
        functools.partial(_conv_prompt_kernel, alpha=alpha),
        grid=(batch, nt),
        in_specs=[pl.BlockSpec((1, tt, d), lambda b, i: (b, i, 0)),
                  pl.BlockSpec((1, CONV_HALO, d), lambda b, i: (b, jnp.maximum(i * hb - 1, 0), 0)),
                  pl.BlockSpec((tt, d), lambda b, i: (b * nt + i, 0)),
                  pl.BlockSpec((1, CONV_WIDTH, d), lambda b, i: (a, 0, 0)),
                  vec(), vec(), vec(),
                  pl.BlockSpec((1, d, d), lambda b, i: (a, 0, 0)),
                  vec(),
                  pl.BlockSpec((1, 1, 1, d), lambda b, i: (li, 0, 0, 0)),
                  pl.BlockSpec((1, 1, 1, d), lambda b, i: (li, 0, 0, 0))],
        out_specs=pl.BlockSpec((tt, d), lambda b, i: (b * nt + i, 0)),
        out_shape=jax.ShapeDtypeStruct((t, d), F32),
        scratch_shapes=[pltpu.VMEM((CONV_HALO + tt, d), F32), pltpu.VMEM((tt, d), F32)],
        compiler_params=_params("parallel", "arbitrary"),
        name="conv_prompt",
    )(u3, u3, x, w_dw, b_dw, cn_g, cn_b, w_pw2, b_pw2, ln_g, ln_b)


def _conv_sample_kernel(u_ref, st_ref, x_ref, wdw_ref, bdw_ref, cng_ref, cnb_ref, w2_ref,
                        b2_ref, lng_ref, lnb_ref, o_ref, ns_ref, win_ref, *, alpha):
    nb, s, d = u_ref.shape
    hist = CONV_WIDTH - 1
    win_ref[:, 0:hist, :] = st_ref[0]
    win_ref[:, hist:hist + s, :] = u_ref[...]
    ns_ref[0] = win_ref[:, s:s + hist, :]
    acc = wdw_ref[0, 0:1, :] * win_ref[:, 0:s, :]
    for j in range(1, CONV_WIDTH):
        acc = acc + wdw_ref[0, j:j + 1, :] * win_ref[:, j:j + s, :]
    y = acc.reshape(nb * s, d)
    o_ref[...] = _conv_tail(y, x_ref[...], bdw_ref, cng_ref, cnb_ref, w2_ref, b2_ref,
                            lng_ref, lnb_ref, alpha)


def _conv_sample(u, x, state_conv, wts, a, li, alpha):
    w_dw, b_dw, cn_g, cn_b, w_pw2, b_pw2, ln_g, ln_b = wts
    t, d = x.shape
    nbatch = state_conv.shape[1]
    s = t // nbatch
    hist = CONV_WIDTH - 1
    nb = min(8, nbatch)
    vec = lambda: pl.BlockSpec((1, 1, d), lambda i: (a, 0, 0))
    x_new, st_new = pl.pallas_call(
        functools.partial(_conv_sample_kernel, alpha=alpha),
        grid=(nbatch // nb,),
        in_specs=[pl.BlockSpec((nb, s, d), lambda i: (i, 0, 0)),
                  pl.BlockSpec((1, nb, hist, d), lambda i: (a, i, 0, 0)),
                  pl.BlockSpec((nb * s, d), lambda i: (i, 0)),
                  pl.BlockSpec((1, CONV_WIDTH, d), lambda i: (a, 0, 0)),
                  vec(), vec(), vec(),
                  pl.BlockSpec((1, d, d), lambda i: (a, 0, 0)),
                  vec(),
                  pl.BlockSpec((1, 1, 1, d), lambda i: (li, 0, 0, 0)),
                  pl.BlockSpec((1, 1, 1, d), lambda i: (li, 0, 0, 0))],
        out_specs=[pl.BlockSpec((nb * s, d), lambda i: (i, 0)),
                   pl.BlockSpec((1, nb, hist, d), lambda i: (0, i, 0, 0))],
        out_shape=[jax.ShapeDtypeStruct((t, d), F32),
                   jax.ShapeDtypeStruct((1, nbatch, hist, d), F32)],
        scratch_shapes=[pltpu.VMEM((nb, hist + s, d), F32)],
        compiler_params=_params("parallel"),
        name="conv_sample",
    )(u.reshape(nbatch, s, d), state_conv, x, w_dw, b_dw, cn_g, cn_b, w_pw2, b_pw2, ln_g, ln_b)
    return x_new, st_new[0]


def _ffn_kernel(x_ref, c_ref, wg_ref, wu_ref, wd_ref, lng_ref, lnb_ref, o_ref, acc_ref, *,
                alpha, weighted):
    e = pl.program_id(1)
    k = pl.program_id(2)

    @pl.when((e == 0) & (k == 0))
    def _():
        acc_ref[...] = jnp.zeros_like(acc_ref)

    x = x_ref[...].astype(BF16)
    g = _dot(x, wg_ref[0, 0])
    u = _dot(x, wu_ref[0, 0])
    h = (g * _sigmoid(g) * u).astype(BF16)
    y = _dot(h, wd_ref[0, 0])
    if weighted:
        c = c_ref[...]
        lane = lax.broadcasted_iota(jnp.int32, c.shape, 1)
        y = y * jnp.sum(jnp.where(lane == e, c, 0.0), axis=1, keepdims=True)
    acc_ref[...] += y

    @pl.when((e == pl.num_programs(1) - 1) & (k == pl.num_programs(2) - 1))
    def _():
        o_ref[...] = _layer_norm(alpha * x_ref[...] + acc_ref[...], lng_ref[0, 0], lnb_ref[0, 0])


def _ffn(x, combine, w_gu, w_down, ln_g, ln_b, a, li, alpha):
    t, d = x.shape
    n_exp, f = w_down.shape[1], w_down.shape[2]
    tm = _token_tile(t)
    tf = min(FF_TILE, f)
    nf = f // tf
    weighted = combine is not None
    if not weighted:
        combine = jnp.ones((t, 1), F32)
    return pl.pallas_call(
        functools.partial(_ffn_kernel, alpha=alpha, weighted=weighted),
        grid=(t // tm, n_exp, nf),
        in_specs=[pl.BlockSpec((tm, d), lambda i, e, k: (i, 0)),
                  pl.BlockSpec((tm, combine.shape[1]), lambda i, e, k: (i, 0)),
                  pl.BlockSpec((1, 1, d, tf), lambda i, e, k: (a, e, 0, k)),
                  pl.BlockSpec((1, 1, d, tf), lambda i, e, k: (a, e, 0, k + nf)),
                  pl.BlockSpec((1, 1, tf, d), lambda i, e, k: (a, e, k, 0)),
                  pl.BlockSpec((1, 1, 1, d), lambda i, e, k: (li, 1, 0, 0)),
                  pl.BlockSpec((1, 1, 1, d), lambda i, e, k: (li, 1, 0, 0))],
        out_specs=pl.BlockSpec((tm, d), lambda i, e, k: (i, 0)),
        out_shape=jax.ShapeDtypeStruct((t, d), F32),
        scratch_shapes=[pltpu.VMEM((tm, d), F32)],
        compiler_params=_params("parallel", "arbitrary", "arbitrary"),
        name="ffn_weighted" if weighted else "ffn_dense",
    )(x, combine, w_gu, w_gu, w_down, ln_g, ln_b)


def _router_kernel(x_ref, wt_ref, b_ref, c_ref):
    x = x_ref[...]
    n_exp = wt_ref.shape[1]
    lane = lax.broadcasted_iota(jnp.int32, (x.shape[0], n_exp), 1).astype(F32)
    logits = jnp.zeros((x.shape[0], n_exp), F32) + b_ref[0]
    for e in range(n_exp):
        col = jnp.sum(x * wt_ref[0, e:e + 1, :], axis=1, keepdims=True)
        logits = logits + jnp.where(lane == e, col, 0.0)
    m1 = jnp.max(logits, axis=1, keepdims=True)
    i1 = jnp.min(jnp.where(logits == m1, lane, float(n_exp)), axis=1, keepdims=True)
    rest = jnp.where(lane == i1, -jnp.inf, logits)
    m2 = jnp.max(rest, axis=1, keepdims=True)
    i2 = jnp.min(jnp.where(rest == m2, lane, float(n_exp)), axis=1, keepdims=True)
    e2 = jnp.exp(m2 - m1)
    g1 = 1.0 / (1.0 + e2)
    g2 = e2 / (1.0 + e2)
    c_ref[...] = jnp.where(lane == i1, g1, 0.0) + jnp.where(lane == i2, g2, 0.0)


def _router(x, w_router_t, b_router, a):
    t, d = x.shape
    n_exp = w_router_t.shape[1]
    tm = min(512, t)
    return pl.pallas_call(
        _router_kernel,
        grid=(t // tm,),
        in_specs=[pl.BlockSpec((tm, d), lambda i: (i, 0)),
                  pl.BlockSpec((1, n_exp, d), lambda i: (a, 0, 0)),
                  pl.BlockSpec((1, 1, n_exp), lambda i: (a, 0, 0))],
        out_specs=pl.BlockSpec((tm, n_exp), lambda i: (i, 0)),
        out_shape=jax.ShapeDtypeStruct((t, n_exp), F32),
        compiler_params=_params("parallel"),
        name="router",
    )(x, w_router_t, b_router)


def _qkv_kernel(x_ref, w_ref, cos_ref, sin_ref, o_ref):
    j = pl.program_id(1)
    z = _dot(x_ref[...].astype(BF16), w_ref[0])
    width = z.shape[1]

    @pl.when(j < 2 * N_GROUPS)
    def _():
        reps = width // cos_ref.shape[1]
        cos = jnp.concatenate([cos_ref[...]] * reps, axis=1)
        sin = jnp.concatenate([sin_ref[...]] * reps, axis=1)
        lane = lax.broadcasted_iota(jnp.int32, z.shape, 1)
        half = HEAD_DIM // 2
        first = (lane & (HEAD_DIM - 1)) < half
        partner = jnp.where(first, pltpu.roll(z, width - half, 1), pltpu.roll(z, half, 1))
        r = z * cos + partner * sin
        o_ref[...] = r * jnp.where(j < N_GROUPS, HEAD_DIM ** -0.5, 1.0)

    @pl.when(j >= 2 * N_GROUPS)
    def _():
        o_ref[...] = z


def _rope_tables(pos):
    half = HEAD_DIM // 2
    inv = ROPE_THETA ** (-jnp.arange(half, dtype=F32) / half)
    ang = pos.astype(F32)[:, None] * inv[None, :]
    cos, sin = jnp.cos(ang), jnp.sin(ang)
    cos2 = jnp.concatenate([cos, cos], axis=1)
    sin2 = jnp.concatenate([-sin, sin], axis=1)
    return jnp.concatenate([cos2, cos2], axis=1), jnp.concatenate([sin2, sin2], axis=1)


def _qkv(x, w_qkv, pos, a):
    t, d = x.shape
    ncol = w_qkv.shape[2]
    tm = _token_tile(t)
    cos, sin = _rope_tables(pos)
    return pl.pallas_call(
        _qkv_kernel,
        grid=(t // tm, ncol // GROUP_WIDTH),
        in_specs=[pl.BlockSpec((tm, d), lambda i, j: (i, 0)),
                  pl.BlockSpec((1, d, GROUP_WIDTH), lambda i, j: (a, 0, j)),
                  pl.BlockSpec((tm, 2 * HEAD_DIM), lambda i, j: (i, 0)),
                  pl.BlockSpec((tm, 2 * HEAD_DIM), lambda i, j: (i, 0))],
        out_specs=pl.BlockSpec((tm, GROUP_WIDTH), lambda i, j: (i, j)),
        out_shape=jax.ShapeDtypeStruct((t, ncol), F32),
        compiler_params=_params("parallel", "arbitrary"),
        name="qkv_rope",
    )(x, w_qkv, cos, sin)


def _band_attn_kernel(q_ref, kp_ref, kc_ref, vp_ref, vc_ref, o_ref, l_ref):
    n = pl.program_id(2)
    tq = q_ref.shape[0]
    qi = lax.broadcasted_iota(jnp.int32, (tq, 2 * tq), 0) + tq
    ki = lax.broadcasted_iota(jnp.int32, (tq, 2 * tq), 1)
    dist = qi - ki
    mask = (dist >= 0) & (dist <= tq) & ((n > 0) | (ki >= tq))
    q = q_ref[...].astype(BF16)
    k = jnp.concatenate([kp_ref[...], kc_ref[...]], axis=0).astype(BF16)
    v = jnp.concatenate([vp_ref[...], vc_ref[...]], axis=0).astype(BF16)
    for h in range(HEADS):
        sl = slice(h * HEAD_DIM, (h + 1) * HEAD_DIM)
        s = jnp.where(mask, _dot_nt(q[:, sl], k[:, sl]), NEG_INF)
        m = jnp.max(s, axis=1, keepdims=True)
        p = jnp.exp(s - m)
        l = jnp.sum(p, axis=1, keepdims=True)
        o_ref[:, sl] = _dot(p.astype(BF16), v[:, sl]) / l
        l_ref[:, sl] = jnp.broadcast_to(m + jnp.log(l), (tq, HEAD_DIM))


def _band_attn(qkv, batch, g):
    win, dil = GROUPS[g]
    assert win // dil == BAND
    t, ncol = qkv.shape
    seq = t // batch
    assert seq % win == 0
    nblk = seq // dil // BAND
    cb = ncol // GROUP_WIDTH
    view = qkv.reshape(t // dil, dil * ncol)
    row = lambda b, c, n: b * nblk + n
    prev = lambda b, c, n: b * nblk + jnp.maximum(n - 1, 0)
    blk = (BAND, GROUP_WIDTH)
    o, lse = pl.pallas_call(
        _band_attn_kernel,
        grid=(batch, dil, nblk),
        in_specs=[pl.BlockSpec(blk, lambda b, c, n: (row(b, c, n), c * cb + g)),
                  pl.BlockSpec(blk, lambda b, c, n: (prev(b, c, n), c * cb + N_GROUPS + g)),
                  pl.BlockSpec(blk, lambda b, c, n: (row(b, c, n), c * cb + N_GROUPS + g)),
                  pl.BlockSpec(blk, lambda b, c, n: (prev(b, c, n), c * cb + 2 * N_GROUPS + g)),
                  pl.BlockSpec(blk, lambda b, c, n: (row(b, c, n), c * cb + 2 * N_GROUPS + g))],
        out_specs=[pl.BlockSpec(blk, lambda b, c, n: (row(b, c, n), c)),
                   pl.BlockSpec(blk, lambda b, c, n: (row(b, c, n), c))],
        out_shape=[jax.ShapeDtypeStruct((t // dil, dil * GROUP_WIDTH), F32)] * 2,
        compiler_params=_params("parallel", "parallel", "arbitrary"),
        name=f"band_attn_g{g}",
    )(view, view, view, view, view)
    return o.reshape(t, GROUP_WIDTH), lse.reshape(t, GROUP_WIDTH)


def _head_diagonal(x, s_new):
    return jnp.concatenate([jnp.sum(x[s * HEADS:(s + 1) * HEADS], axis=0, keepdims=True)
                            for s in range(s_new)], axis=0)


def _sample_attn_kernel(q_ref, kn_ref, vn_ref, kc_ref, vc_ref, o_ref, l_ref, *, dil):
    s_new = q_ref.shape[0]
    rows = s_new * HEADS
    ncls = min(dil, s_new)
    shift = dil.bit_length() - 1
    width = GROUP_WIDTH
    row = lax.broadcasted_iota(jnp.int32, (rows, width), 0)
    lane = lax.broadcasted_iota(jnp.int32, (rows, width), 1)
    headmask = (lane >> 6) == (row & (HEADS - 1))
    q = q_ref[...]
    qrep = jnp.concatenate([jnp.broadcast_to(q[s:s + 1], (HEADS, width)) for s in range(s_new)], axis=0)
    qbd = jnp.where(headmask, qrep, 0.0).astype(BF16)
    rs = lax.broadcasted_iota(jnp.int32, (rows, BAND), 0) >> 3
    rcls = rs & (dil - 1)
    rr = rs >> shift
    ki = lax.broadcasted_iota(jnp.int32, (rows, BAND), 1)

    s_c = _dot_nt(qbd, kc_ref[0, 0, :, 0:width].astype(BF16))
    for c in range(1, ncls):
        s_c = jnp.where(rcls == c, _dot_nt(qbd, kc_ref[0, 0, :, c * width:(c + 1) * width].astype(BF16)), s_c)
    s_c = jnp.where(ki >= rr, s_c, NEG_INF)

    pad = jnp.zeros((BAND - s_new, width), F32)
    kn = jnp.concatenate([kn_ref[...], pad], axis=0).astype(BF16)
    vn = jnp.concatenate([vn_ref[...], pad], axis=0).astype(BF16)
    s_n = _dot_nt(qbd, kn)
    s_n = jnp.where((ki < s_new) & ((ki & (dil - 1)) == rcls) & (ki <= rs), s_n, NEG_INF)

    m = jnp.maximum(jnp.max(s_c, axis=1, keepdims=True), jnp.max(s_n, axis=1, keepdims=True))
    p_c = jnp.exp(s_c - m)
    p_n = jnp.exp(s_n - m)
    l = jnp.sum(p_c, axis=1, keepdims=True) + jnp.sum(p_n, axis=1, keepdims=True)
    o = _dot(p_n.astype(BF16), vn)
    for c in range(ncls):
        p = p_c if ncls == 1 else jnp.where(rcls == c, p_c, 0.0)
        o = o + _dot(p.astype(BF16), vc_ref[0, 0, :, c * width:(c + 1) * width].astype(BF16))
    o = o / l
    lse = jnp.broadcast_to(m + jnp.log(l), (rows, width))
    o_ref[...] = _head_diagonal(jnp.where(headmask, o, 0.0), s_new)
    l_ref[...] = _head_diagonal(jnp.where(headmask, lse, 0.0), s_new)


def _sample_attn(qkv, cache_k, cache_v, a, g):
    win, dil = GROUPS[g]
    assert win // dil == BAND
    nbatch = cache_k.shape[1]
    assert cache_k.shape[2] == win
    t = qkv.shape[0]
    s_new = t // nbatch
    ncls = min(dil, s_new)
    kview = cache_k.reshape(cache_k.shape[0], nbatch, BAND, dil * GROUP_WIDTH)
    vview = cache_v.reshape(cache_v.shape[0], nbatch, BAND, dil * GROUP_WIDTH)
    new = (s_new, GROUP_WIDTH)
    cblk = (1, 1, BAND, ncls * GROUP_WIDTH)
    return pl.pallas_call(
        functools.partial(_sample_attn_kernel, dil=dil),
        grid=(nbatch,),
        in_specs=[pl.BlockSpec(new, lambda b: (b, g)),
                  pl.BlockSpec(new, lambda b: (b, N_GROUPS + g)),
                  pl.BlockSpec(new, lambda b: (b, 2 * N_GROUPS + g)),
                  pl.BlockSpec(cblk, lambda b: (a, b, 0, 0)),
                  pl.BlockSpec(cblk, lambda b: (a, b, 0, 0))],
        out_specs=[pl.BlockSpec(new, lambda b: (b, 0)), pl.BlockSpec(new, lambda b: (b, 0))],
        out_shape=[jax.ShapeDtypeStruct((t, GROUP_WIDTH), F32)] * 2,
        compiler_params=_params("parallel"),
        name=f"sample_attn_g{g}",
    )(qkv, qkv, qkv, kview, vview)


def _merge_kernel(*refs, alpha):
    o_refs = refs[0:N_GROUPS]
    l_refs = refs[N_GROUPS:2 * N_GROUPS]
    x_ref, wo_ref, lng_ref, lnb_ref, out_ref = refs[2 * N_GROUPS:]
    lses = [r[...] for r in l_refs]
    top = functools.reduce(jnp.maximum, lses)
    es = [jnp.exp(l - top) for l in lses]
    den = functools.reduce(jnp.add, es)
    y = functools.reduce(jnp.add, [(e / den) * r[...] for e, r in zip(es, o_refs)])
    m = _dot(y.astype(BF16), wo_ref[0])
    out_ref[...] = _layer_norm(alpha * x_ref[...] + m, lng_ref[0, 0], lnb_ref[0, 0])


def _merge(outs, lses, x, w_o, ln_g, ln_b, a, li, alpha):
    t, d = x.shape
    tm = min(512, t)
    grp = lambda: pl.BlockSpec((tm, GROUP_WIDTH), lambda i: (i, 0))
    return pl.pallas_call(
        functools.partial(_merge_kernel, alpha=alpha),
        grid=(t // tm,),
        in_specs=[grp() for _ in range(2 * N_GROUPS)] + [
            pl.BlockSpec((tm, d), lambda i: (i, 0)),
            pl.BlockSpec((1, GROUP_WIDTH, d), lambda i: (a, 0, 0)),
            pl.BlockSpec((1, 1, 1, d), lambda i: (li, 0, 0, 0)),
            pl.BlockSpec((1, 1, 1, d), lambda i: (li, 0, 0, 0))],
        out_specs=pl.BlockSpec((tm, d), lambda i: (i, 0)),
        out_shape=jax.ShapeDtypeStruct((t, d), F32),
        compiler_params=_params("parallel"),
        name="merge_wo",
    )(*outs, *lses, x, w_o, ln_g, ln_b)


def kernel(x_prompt, x_sample, state_conv, cache_k_w128, cache_v_w128, cache_k_w512, cache_v_w512, cache_k_w2048, cache_v_w2048, ln_g, ln_b, w_pw1, b_pw1, w_dw, b_dw, cn_g, cn_b, w_pw2, b_pw2, w_qkv, w_o, w_ffn_gu, w_ffn_down, w_router, b_router, w_exp_gu, w_exp_down):
    batch, seq, d = x_prompt.shape
    nbatch, s_new, _ = x_sample.shape
    depth = ln_g.shape[0]
    alpha = (2 * depth) ** 0.25
    caches = ((cache_k_w128, cache_v_w128), (cache_k_w512, cache_v_w512),
              (cache_k_w2048, cache_v_w2048))

    w_pw1_h, w_pw2_h, w_qkv_h, w_o_h = (w.astype(BF16) for w in (w_pw1, w_pw2, w_qkv, w_o))
    w_ffn_gu_h = w_ffn_gu.astype(BF16)[:, None]
    w_ffn_down_h = w_ffn_down.astype(BF16)[:, None]
    w_exp_gu_h = w_exp_gu.astype(BF16)
    w_exp_down_h = w_exp_down.astype(BF16)
    row = lambda v: v[:, None, :]
    conv_wts = (w_dw, row(b_dw), row(cn_g), row(cn_b), w_pw2_h, row(b_pw2))
    ln_g4, ln_b4 = ln_g[:, :, None, :], ln_b[:, :, None, :]
    w_router_t = jnp.swapaxes(w_router, 1, 2)
    b_router3 = row(b_router)

    pos_p = jnp.tile(jnp.arange(seq), batch)
    pos_s = jnp.tile(PAST_LEN + jnp.arange(s_new), nbatch)

    xp = x_prompt.reshape(batch * seq, d)
    xs = x_sample.reshape(nbatch * s_new, d)
    conv_p, conv_s = [], []
    kv_p = [[] for _ in range(2 * N_GROUPS)]
    kv_s_new = [[] for _ in range(2 * N_GROUPS)]
    for li in range(depth):
        a = li // 2
        if li % 2 == 0:
            wts = conv_wts + (ln_g4, ln_b4)
            up = _pw1_glu(xp, w_pw1_h, row(b_pw1), a)
            us = _pw1_glu(xs, w_pw1_h, row(b_pw1), a)
            xp = _conv_prompt(up, xp, batch, wts, a, li, alpha)
            xs, st = _conv_sample(us, xs, state_conv, wts, a, li, alpha)
            conv_p.append(up.reshape(batch, seq, d)[:, seq - (CONV_WIDTH - 1):])
            conv_s.append(st)
            xp = _ffn(xp, None, w_ffn_gu_h, w_ffn_down_h, ln_g4, ln_b4, a, li, alpha)
            xs = _ffn(xs, None, w_ffn_gu_h, w_ffn_down_h, ln_g4, ln_b4, a, li, alpha)
        else:
            qkv_p = _qkv(xp, w_qkv_h, pos_p, a)
            qkv_s = _qkv(xs, w_qkv_h, pos_s, a)
            outs_p, lses_p, outs_s, lses_s = [], [], [], []
            for g, (win, _) in enumerate(GROUPS):
                o, l = _band_attn(qkv_p, batch, g)
                outs_p.append(o)
                lses_p.append(l)
                o, l = _sample_attn(qkv_s, caches[g][0], caches[g][1], a, g)
                outs_s.append(o)
                lses_s.append(l)
                keep = min(win, seq)
                for j in range(2):
                    c0 = ((1 + j) * N_GROUPS + g) * GROUP_WIDTH
                    new_p = qkv_p[:, c0:c0 + GROUP_WIDTH].reshape(batch, seq, HEADS, HEAD_DIM)
                    kv_p[2 * g + j].append(new_p[:, seq - keep:])
                    new_s = qkv_s[:, c0:c0 + GROUP_WIDTH].reshape(nbatch, s_new, HEADS, HEAD_DIM)
                    kv_s_new[2 * g + j].append(new_s)
            xp = _merge(outs_p, lses_p, xp, w_o_h, ln_g4, ln_b4, a, li, alpha)
            xs = _merge(outs_s, lses_s, xs, w_o_h, ln_g4, ln_b4, a, li, alpha)
            cp = _router(xp, w_router_t, b_router3, a)
            cs = _router(xs, w_router_t, b_router3, a)
            xp = _ffn(xp, cp, w_exp_gu_h, w_exp_down_h, ln_g4, ln_b4, a, li, alpha)
            xs = _ffn(xs, cs, w_exp_gu_h, w_exp_down_h, ln_g4, ln_b4, a, li, alpha)

    kv_p_out = [jnp.stack(t) for t in kv_p]
    kv_s_out = []
    for g in range(N_GROUPS):
        for j in range(2):
            new = jnp.stack(kv_s_new[2 * g + j])
            kv_s_out.append(jnp.concatenate([caches[g][j][:, :, s_new:], new], axis=2))
    return (xp.reshape(batch, seq, d), xs.reshape(nbatch, s_new, d),
            jnp.stack(conv_p), jnp.stack(conv_s), *kv_p_out, *kv_s_out)
```

```python
import functools

import jax
import jax.numpy as jnp
from jax import lax
from jax.experimental import pallas as pl
from jax.experimental.pallas import tpu as pltpu

GROUPS = ((128, 1), (512, 4), (2048, 16))
N_GROUPS = len(GROUPS)
HEADS = 8
HEAD_DIM = 64
GROUP_WIDTH = HEADS * HEAD_DIM
BAND = 128
CONV_WIDTH = 31
CONV_HIST = CONV_WIDTH - 1
CONV_HALO = 32
PAST_LEN = 2048
ROPE_THETA = 10000.0
LN_EPS = 1e-5
NEG_INF = -1e30
LANES = 128

VMEM_LIMIT_BYTES = 56 * 1024 * 1024
TOKEN_TILE = 1024
FF_TILE = 512
MOE_TOKEN_TILE = 2048
MOE_FF_TILE = 896
MOE_CHUNK = 256
MOE_SLOTS = 128
SLOT_SHIFT = MOE_SLOTS.bit_length() - 1
ROUTER_TILE = 512
CACHE_BLOCK_BYTES = 2 * 1024 * 1024

F32 = jnp.float32
BF16 = jnp.bfloat16


def _params(*semantics):
    return pltpu.CompilerParams(dimension_semantics=semantics,
                                vmem_limit_bytes=VMEM_LIMIT_BYTES)


def _layer_norm(x, g, b):
    mu = jnp.mean(x, axis=-1, keepdims=True)
    xc = x - mu
    var = jnp.mean(xc * xc, axis=-1, keepdims=True)
    return xc * lax.rsqrt(var + LN_EPS) * g + b


def _sigmoid(x):
    return 1.0 / (1.0 + jnp.exp(-x))


def _dot(a, b):
    return jnp.dot(a, b, preferred_element_type=F32)


def _dot_nt(a, b):
    return lax.dot_general(a, b, (((1,), (1,)), ((), ())), preferred_element_type=F32)


def _token_tile(t):
    return min(TOKEN_TILE, t)


def _ln_spec(d, li, which, nargs):
    idx = (li, which, 0, 0)
    return pl.BlockSpec((1, 1, 1, d), {1: lambda i: idx, 2: lambda i, j: idx,
                                       3: lambda i, j, k: idx}[nargs])


def _pw1_glu_kernel(x_ref, w_ref, b_ref, u_ref):
    d = u_ref.shape[-1]
    x = x_ref[...].astype(BF16)
    a = _dot(x, w_ref[0, :, :d]) + b_ref[0, :, :d]
    g = _dot(x, w_ref[0, :, d:]) + b_ref[0, :, d:]
    u_ref[...] = a * _sigmoid(g)


def _pw1_glu(x, w_pw1, b_pw1, a):
    t, d = x.shape
    tm = _token_tile(t)
    return pl.pallas_call(
        _pw1_glu_kernel,
        grid=(t // tm,),
        in_specs=[pl.BlockSpec((tm, d), lambda i: (i, 0)),
                  pl.BlockSpec((1, d, 2 * d), lambda i: (a, 0, 0)),
                  pl.BlockSpec((1, 1, 2 * d), lambda i: (a, 0, 0))],
        out_specs=pl.BlockSpec((tm, d), lambda i: (i, 0)),
        out_shape=jax.ShapeDtypeStruct((t, d), F32),
        compiler_params=_params("parallel"),
        name="pw1_glu",
    )(x, w_pw1, b_pw1)


def _conv_tail(y, x, bdw_ref, cng_ref, cnb_ref, w2_ref, b2_ref, lng_ref, lnb_ref, alpha):
    y = y + bdw_ref[0]
    y = _layer_norm(y, cng_ref[0], cnb_ref[0])
    y = y * _sigmoid(y)
    m = _dot(y.astype(BF16), w2_ref[0]) + b2_ref[0]
    return _layer_norm(alpha * x + m, lng_ref[0, 0], lnb_ref[0, 0])


CONV_ROW_CHUNK = 32


def _conv_prompt_kernel(um_ref, uh_ref, x_ref, wdw_ref, bdw_ref, cng_ref, cnb_ref, w2_ref,
                        b2_ref, lng_ref, lnb_ref, o_ref, win_ref, y_ref, *, alpha):
    i = pl.program_id(1)
    tt = um_ref.shape[1]
    win_ref[0:CONV_HALO, :] = jnp.where(i > 0, uh_ref[0], 0.0)
    win_ref[CONV_HALO:CONV_HALO + tt, :] = um_ref[0]
    lead = CONV_HALO - CONV_HIST
    for r in range(0, tt, CONV_ROW_CHUNK):
        acc = wdw_ref[0, 0:1, :] * win_ref[r + lead:r + lead + CONV_ROW_CHUNK, :]
        for j in range(1, CONV_WIDTH):
            acc = acc + wdw_ref[0, j:j + 1, :] * win_ref[r + lead + j:r + lead + j + CONV_ROW_CHUNK, :]
        y_ref[r:r + CONV_ROW_CHUNK, :] = acc
    o_ref[...] = _conv_tail(y_ref[...], x_ref[...], bdw_ref, cng_ref, cnb_ref, w2_ref, b2_ref,
                            lng_ref, lnb_ref, alpha)


def _conv_prompt(u, x, batch, wts, a, li, alpha):
    w_dw, b_dw, cn_g, cn_b, w_pw2, b_pw2, ln_g, ln_b = wts
    t, d = x.shape
    seq = t // batch
    tt = min(512, seq)
    nt = seq // tt
    hb = tt // CONV_HALO
    u3 = u.reshape(batch, seq, d)
    vec = lambda: pl.BlockSpec((1, 1, d), lambda b, i: (a, 0, 0))
    return pl.pallas_call(
        functools.partial(_conv_prompt_kernel, alpha=alpha),
        grid=(batch, nt),
        in_specs=[pl.BlockSpec((1, tt, d), lambda b, i: (b, i, 0)),
                  pl.BlockSpec((1, CONV_HALO, d), lambda b, i: (b, jnp.maximum(i * hb - 1, 0), 0)),
                  pl.BlockSpec((tt, d), lambda b, i: (b * nt + i, 0)),
                  pl.BlockSpec((1, CONV_WIDTH, d), lambda b, i: (a, 0, 0)),
                  vec(), vec(), vec(),
                  pl.BlockSpec((1, d, d), lambda b, i: (a, 0, 0)),
                  vec(),
                  _ln_spec(d, li, 0, 2), _ln_spec(d, li, 0, 2)],
        out_specs=pl.BlockSpec((tt, d), lambda b, i: (b * nt + i, 0)),
        out_shape=jax.ShapeDtypeStruct((t, d), F32),
        scratch_shapes=[pltpu.VMEM((CONV_HALO + tt, d), F32), pltpu.VMEM((tt, d), F32)],
        compiler_params=_params("parallel", "arbitrary"),
        name="conv_prompt",
    )(u3, u3, x, w_dw, b_dw, cn_g, cn_b, w_pw2, b_pw2, ln_g, ln_b)


def _conv_sample_kernel(u_ref, st_ref, x_ref, wdw_ref, bdw_ref, cng_ref, cnb_ref, w2_ref,
                        b2_ref, lng_ref, lnb_ref, o_ref, ns_ref, win_ref, *, alpha):
    s, nb, d = u_ref.shape
    win_ref[0:CONV_HIST] = st_ref[0]
    win_ref[CONV_HIST:CONV_HIST + s] = u_ref[...]
    ns_ref[0] = win_ref[s:s + CONV_HIST]
    acc = wdw_ref[0, 0:1, :] * win_ref[0:s]
    for j in range(1, CONV_WIDTH):
        acc = acc + wdw_ref[0, j:j + 1, :] * win_ref[j:j + s]
    out = _conv_tail(acc.reshape(s * nb, d), x_ref[...].reshape(s * nb, d), bdw_ref, cng_ref,
                     cnb_ref, w2_ref, b2_ref, lng_ref, lnb_ref, alpha)
    o_ref[...] = out.reshape(s, nb, d)


def _conv_sample(u, x, state_t, wts, a, li, alpha):
    w_dw, b_dw, cn_g, cn_b, w_pw2, b_pw2, ln_g, ln_b = wts
    t, d = x.shape
    nbatch = state_t.shape[2]
    s = t // nbatch
    nb = min(16, nbatch)
    vec = lambda: pl.BlockSpec((1, 1, d), lambda i: (a, 0, 0))
    tok = lambda: pl.BlockSpec((s, nb, d), lambda i: (0, i, 0))
    x_new, st_new = pl.pallas_call(
        functools.partial(_conv_sample_kernel, alpha=alpha),
        grid=(nbatch // nb,),
        in_specs=[tok(),
                  pl.BlockSpec((1, CONV_HIST, nb, d), lambda i: (a, 0, i, 0)),
                  tok(),
                  pl.BlockSpec((1, CONV_WIDTH, d), lambda i: (a, 0, 0)),
                  vec(), vec(), vec(),
                  pl.BlockSpec((1, d, d), lambda i: (a, 0, 0)),
                  vec(),
                  _ln_spec(d, li, 0, 1), _ln_spec(d, li, 0, 1)],
        out_specs=[tok(), pl.BlockSpec((1, CONV_HIST, nb, d), lambda i: (0, 0, i, 0))],
        out_shape=[jax.ShapeDtypeStruct((s, nbatch, d), F32),
                   jax.ShapeDtypeStruct((1, CONV_HIST, nbatch, d), F32)],
        scratch_shapes=[pltpu.VMEM((CONV_HIST + s, nb, d), F32)],
        compiler_params=_params("parallel"),
        name="conv_sample",
    )(u.reshape(s, nbatch, d), state_t, x.reshape(s, nbatch, d), w_dw, b_dw, cn_g, cn_b,
      w_pw2, b_pw2, ln_g, ln_b)
    return x_new.reshape(t, d), st_new


def _ffn_kernel(x_ref, wg_ref, wu_ref, wd_ref, lng_ref, lnb_ref, o_ref, acc_ref, *, alpha):
    k = pl.program_id(1)

    @pl.when(k == 0)
    def _():
        acc_ref[...] = jnp.zeros_like(acc_ref)

    x = x_ref[...].astype(BF16)
    g = _dot(x, wg_ref[0])
    u = _dot(x, wu_ref[0])
    h = (g * _sigmoid(g) * u).astype(BF16)
    acc_ref[...] += _dot(h, wd_ref[0])

    @pl.when(k == pl.num_programs(1) - 1)
    def _():
        o_ref[...] = _layer_norm(alpha * x_ref[...] + acc_ref[...], lng_ref[0, 0], lnb_ref[0, 0])


def _ffn(x, w_gu, w_down, ln_g, ln_b, a, li, alpha):
    t, d = x.shape
    f = w_down.shape[1]
    tm = _token_tile(t)
    tf = min(FF_TILE, f)
    nf = f // tf
    return pl.pallas_call(
        functools.partial(_ffn_kernel, alpha=alpha),
        grid=(t // tm, nf),
        in_specs=[pl.BlockSpec((tm, d), lambda i, k: (i, 0)),
                  pl.BlockSpec((1, d, tf), lambda i, k: (a, 0, k)),
                  pl.BlockSpec((1, d, tf), lambda i, k: (a, 0, k + nf)),
                  pl.BlockSpec((1, tf, d), lambda i, k: (a, k, 0)),
                  _ln_spec(d, li, 1, 2), _ln_spec(d, li, 1, 2)],
        out_specs=pl.BlockSpec((tm, d), lambda i, k: (i, 0)),
        out_shape=jax.ShapeDtypeStruct((t, d), F32),
        scratch_shapes=[pltpu.VMEM((tm, d), F32)],
        compiler_params=_params("parallel", "arbitrary"),
        name="ffn_dense",
    )(x, w_gu, w_gu, w_down, ln_g, ln_b)


def _router_kernel(x_ref, wt_ref, b_ref, c_ref, r_ref, count_ref, *, tiles_per_group):
    i = pl.program_id(0)
    x = x_ref[...]
    tr = x.shape[0]
    n_exp = wt_ref.shape[1]
    lane = lax.broadcasted_iota(jnp.int32, (tr, LANES), 1).astype(F32)
    logits = jnp.where(lane < n_exp, b_ref[0], -jnp.inf)
    for e in range(n_exp):
        col = jnp.sum(x * wt_ref[0, e:e + 1, :], axis=1, keepdims=True)
        logits = logits + jnp.where(lane == e, col, 0.0)
    m1 = jnp.max(logits, axis=1, keepdims=True)
    i1 = jnp.min(jnp.where(logits == m1, lane, float(LANES)), axis=1, keepdims=True)
    rest = jnp.where(lane == i1, -jnp.inf, logits)
    m2 = jnp.max(rest, axis=1, keepdims=True)
    i2 = jnp.min(jnp.where(rest == m2, lane, float(LANES)), axis=1, keepdims=True)
    e2 = jnp.exp(m2 - m1)
    g1 = 1.0 / (1.0 + e2)
    g2 = e2 / (1.0 + e2)
    c_ref[...] = jnp.where(lane == i1, g1, 0.0) + jnp.where(lane == i2, g2, 0.0)

    @pl.when(i % tiles_per_group == 0)
    def _():
        count_ref[...] = jnp.zeros_like(count_ref)

    member = (lane == i1) | (lane == i2)
    below = (lax.broadcasted_iota(jnp.int32, (tr, tr), 0) >
             lax.broadcasted_iota(jnp.int32, (tr, tr), 1))
    earlier = _dot(below.astype(BF16), member.astype(BF16))
    r_ref[...] = jnp.where(member, earlier + count_ref[...], -1.0)
    count_ref[...] += jnp.sum(member.astype(F32), axis=0, keepdims=True)


def _router(x, w_router_t, b_router, a, moe_tile):
    t, d = x.shape
    n_exp = w_router_t.shape[1]
    tr = min(ROUTER_TILE, t)
    b_pad = jnp.pad(b_router, ((0, 0), (0, 0), (0, LANES - n_exp)))
    return pl.pallas_call(
        functools.partial(_router_kernel, tiles_per_group=moe_tile // tr),
        grid=(t // tr,),
        in_specs=[pl.BlockSpec((tr, d), lambda i: (i, 0)),
                  pl.BlockSpec((1, n_exp, d), lambda i: (a, 0, 0)),
                  pl.BlockSpec((1, 1, LANES), lambda i: (a, 0, 0))],
        out_specs=[pl.BlockSpec((tr, LANES), lambda i: (i, 0)),
                   pl.BlockSpec((tr, LANES), lambda i: (i, 0))],
        out_shape=[jax.ShapeDtypeStruct((t, LANES), F32)] * 2,
        scratch_shapes=[pltpu.VMEM((1, LANES), F32)],
        compiler_params=_params("arbitrary"),
        name="router",
    )(x, w_router_t, b_pad)


def _slot_rows(q):
    return pl.ds(pl.multiple_of(q * MOE_SLOTS, MOE_SLOTS), MOE_SLOTS)


def _moe_kernel(pre_ref, xb_ref, rt_ref, r_ref, c_ref, wg_ref, wu_ref, wd_ref, y_ref,
                xs_ref, ys_ref, *, n_chunks, chunk):
    i, e, k = pl.program_id(0), pl.program_id(1), pl.program_id(2)
    base = (i * pl.num_programs(1) + e) * (n_chunks + 1)
    n_sel = pre_ref[base + n_chunks]
    n_blocks = (n_sel + (MOE_SLOTS - 1)) >> SLOT_SHIFT

    def chunk_blocks(c):
        s0, s1 = pre_ref[base + c], pre_ref[base + c + 1]
        lo = s0 >> SLOT_SHIFT
        hi = (s1 + (MOE_SLOTS - 1)) >> SLOT_SHIFT
        return lo, jnp.where(s1 > s0, hi, lo)

    @pl.when((e == 0) & (k == 0))
    def _():
        y_ref[...] = jnp.zeros_like(y_ref)

    @pl.when(k == 0)
    def _():
        def clear(q, carry):
            xs_ref[_slot_rows(q), :] = jnp.zeros((MOE_SLOTS, xs_ref.shape[1]), BF16)
            return carry
        lax.fori_loop(0, n_blocks, clear, 0)
        for c in range(n_chunks):
            rank_row = rt_ref[0, :, c * chunk:(c + 1) * chunk]
            xc = xb_ref[c * chunk:(c + 1) * chunk, :]
            lo, hi = chunk_blocks(c)

            def gather(q, carry, rank_row=rank_row, xc=xc):
                slot = q * MOE_SLOTS + lax.broadcasted_iota(jnp.int32, (MOE_SLOTS, chunk), 0)
                onehot = (slot.astype(F32) == rank_row).astype(BF16)
                xs_ref[_slot_rows(q), :] += _dot(onehot, xc).astype(BF16)
                return carry
            lax.fori_loop(lo, hi, gather, 0)

    def ffn(q, carry):
        rows = _slot_rows(q)
        xq = xs_ref[rows, :]
        g = _dot(xq, wg_ref[0, 0])
        u = _dot(xq, wu_ref[0, 0])
        h = (g * _sigmoid(g) * u).astype(BF16)
        part = _dot(h, wd_ref[0, 0])

        @pl.when(k == 0)
        def _():
            ys_ref[rows, :] = part

        @pl.when(k > 0)
        def _():
            ys_ref[rows, :] += part
        return carry
    lax.fori_loop(0, n_blocks, ffn, 0)

    @pl.when(k == pl.num_programs(2) - 1)
    def _():
        for c in range(n_chunks):
            tok = slice(c * chunk, (c + 1) * chunk)
            lane = lax.broadcasted_iota(jnp.int32, (chunk, LANES), 1)
            rank_col = jnp.sum(jnp.where(lane == e, r_ref[tok, :], 0.0), axis=1, keepdims=True)
            gate_col = jnp.sum(jnp.where(lane == e, c_ref[tok, :], 0.0), axis=1, keepdims=True)
            lo, hi = chunk_blocks(c)

            def scatter(q, carry, tok=tok, rank_col=rank_col, gate_col=gate_col):
                slot = q * MOE_SLOTS + lax.broadcasted_iota(jnp.int32, (chunk, MOE_SLOTS), 1)
                onehot = (rank_col == slot.astype(F32)).astype(BF16)
                y_ref[tok, :] += gate_col * _dot(onehot, ys_ref[_slot_rows(q), :].astype(BF16))
                return carry
            lax.fori_loop(lo, hi, scatter, 0)


def _moe(xb, combine, rank, w_gu, w_down, a):
    t, d = xb.shape
    n_exp, f = w_down.shape[1], w_down.shape[2]
    tm = min(MOE_TOKEN_TILE, t)
    nt = t // tm
    chunk = min(MOE_CHUNK, tm)
    n_chunks = tm // chunk
    cap = -(-tm // MOE_SLOTS) * MOE_SLOTS
    tf = MOE_FF_TILE if f % MOE_FF_TILE == 0 else min(FF_TILE, f)
    nf = f // tf
    member = (rank[:, :n_exp] >= 0).astype(jnp.int32)
    per_chunk = member.reshape(nt, n_chunks, chunk, n_exp).sum(axis=2)
    pre = jnp.concatenate([jnp.zeros((nt, 1, n_exp), jnp.int32), jnp.cumsum(per_chunk, axis=1)], axis=1)
    pre = jnp.swapaxes(pre, 1, 2).reshape(-1)
    rank_t = jnp.swapaxes(rank[:, :n_exp], 0, 1)[:, None, :]
    grid_spec = pltpu.PrefetchScalarGridSpec(
        num_scalar_prefetch=1,
        grid=(nt, n_exp, nf),
        in_specs=[pl.BlockSpec((tm, d), lambda i, e, k, pre: (i, 0)),
                  pl.BlockSpec((1, 1, tm), lambda i, e, k, pre: (e, 0, i)),
                  pl.BlockSpec((tm, LANES), lambda i, e, k, pre: (i, 0)),
                  pl.BlockSpec((tm, LANES), lambda i, e, k, pre: (i, 0)),
                  pl.BlockSpec((1, 1, d, tf), lambda i, e, k, pre: (a, e, 0, k)),
                  pl.BlockSpec((1, 1, d, tf), lambda i, e, k, pre: (a, e, 0, k + nf)),
                  pl.BlockSpec((1, 1, tf, d), lambda i, e, k, pre: (a, e, k, 0))],
        out_specs=pl.BlockSpec((tm, d), lambda i, e, k, pre: (i, 0)),
        scratch_shapes=[pltpu.VMEM((cap, d), BF16), pltpu.VMEM((cap, d), F32)])
    return pl.pallas_call(
        functools.partial(_moe_kernel, n_chunks=n_chunks, chunk=chunk),
        grid_spec=grid_spec,
        out_shape=jax.ShapeDtypeStruct((t, d), F32),
        compiler_params=_params("parallel", "arbitrary", "arbitrary"),
        name="moe_sparse",
    )(pre, xb, rank_t, rank, combine, w_gu, w_gu, w_down)


def _residual_ln_kernel(x_ref, y_ref, lng_ref, lnb_ref, o_ref, *, alpha):
    o_ref[...] = _layer_norm(alpha * x_ref[...] + y_ref[...], lng_ref[0, 0], lnb_ref[0, 0])


def _residual_ln(x, y, ln_g, ln_b, li, alpha):
    t, d = x.shape
    tm = _token_tile(t)
    tok = lambda: pl.BlockSpec((tm, d), lambda i: (i, 0))
    return pl.pallas_call(
        functools.partial(_residual_ln_kernel, alpha=alpha),
        grid=(t // tm,),
        in_specs=[tok(), tok(), _ln_spec(d, li, 1, 1), _ln_spec(d, li, 1, 1)],
        out_specs=tok(),
        out_shape=jax.ShapeDtypeStruct((t, d), F32),
        compiler_params=_params("parallel"),
        name="residual_ln",
    )(x, y, ln_g, ln_b)


def _qkv_kernel(x_ref, w_ref, cos_ref, sin_ref, *refs, dils):
    o_refs, z_ref, s_ref = refs[:N_GROUPS], refs[N_GROUPS], refs[N_GROUPS + 1]
    j = pl.program_id(1)
    g = j // 3
    kind = j - 3 * g
    z = _dot(x_ref[...].astype(BF16), w_ref[0])
    tm, width = z.shape

    @pl.when(kind < 2)
    def _():
        reps = width // cos_ref.shape[1]
        cos = jnp.concatenate([cos_ref[...]] * reps, axis=1)
        sin = jnp.concatenate([sin_ref[...]] * reps, axis=1)
        lane = lax.broadcasted_iota(jnp.int32, z.shape, 1)
        half = HEAD_DIM // 2
        first = (lane & (HEAD_DIM - 1)) < half
        partner = jnp.where(first, pltpu.roll(z, width - half, 1), pltpu.roll(z, half, 1))
        r = z * cos + partner * sin
        z_ref[...] = r * jnp.where(kind == 0, HEAD_DIM ** -0.5, 1.0)

    @pl.when(kind == 2)
    def _():
        z_ref[...] = z

    for gi, dil in enumerate(dils):
        @pl.when(g == gi)
        def _(gi=gi, dil=dil):
            if dil == 1:
                o_refs[gi][0] = z_ref[...]
            else:
                for t in range(width // LANES):
                    s_ref[t] = z_ref[:, t * LANES:(t + 1) * LANES]
                for c in range(dil):
                    for t in range(width // LANES):
                        o_refs[gi][c, :, t * LANES:(t + 1) * LANES] = (
                            s_ref[t, pl.ds(c, tm // dil, stride=dil), :])


def _rope_tables(pos):
    half = HEAD_DIM // 2
    inv = ROPE_THETA ** (-jnp.arange(half, dtype=F32) / half)
    ang = pos.astype(F32)[:, None] * inv[None, :]
    cos, sin = jnp.cos(ang), jnp.sin(ang)
    cos2 = jnp.concatenate([cos, cos], axis=1)
    sin2 = jnp.concatenate([-sin, sin], axis=1)
    return jnp.concatenate([cos2, cos2], axis=1), jnp.concatenate([sin2, sin2], axis=1)


def _qkv(x, w_qkv, pos, a, nseq, dils):
    t, d = x.shape
    seq = t // nseq
    tm = min(512, seq)
    tiles = seq // tm
    cos, sin = _rope_tables(pos)

    def out_spec(gi, dil):
        return pl.BlockSpec(
            (None, dil, tm // dil, GROUP_WIDTH),
            lambda i, j: (i // tiles, 0, i % tiles, jnp.clip(j - 3 * gi, 0, 2)))

    def w_col(i, j):
        return (a, 0, (j % 3) * N_GROUPS + j // 3)

    return pl.pallas_call(
        functools.partial(_qkv_kernel, dils=dils),
        grid=(t // tm, 3 * N_GROUPS),
        in_specs=[pl.BlockSpec((tm, d), lambda i, j: (i, 0)),
                  pl.BlockSpec((1, d, GROUP_WIDTH), w_col),
                  pl.BlockSpec((tm, 2 * HEAD_DIM), lambda i, j: (i, 0)),
                  pl.BlockSpec((tm, 2 * HEAD_DIM), lambda i, j: (i, 0))],
        out_specs=[out_spec(gi, dil) for gi, dil in enumerate(dils)],
        out_shape=[jax.ShapeDtypeStruct((nseq, dil, seq // dil, 3 * GROUP_WIDTH), F32)
                   for dil in dils],
        scratch_shapes=[pltpu.VMEM((tm, GROUP_WIDTH), F32),
                        pltpu.VMEM((GROUP_WIDTH // LANES, tm, LANES), F32)],
        compiler_params=_params("parallel", "arbitrary"),
        name="qkv_rope",
    )(x, w_qkv, cos, sin)


def _band_attn_kernel(q_ref, kp_ref, kc_ref, vp_ref, vc_ref, o_ref, l_ref, *, dil):
    n, c = pl.program_id(1), pl.program_id(2)
    tq = q_ref.shape[0]
    qi = lax.broadcasted_iota(jnp.int32, (tq, 2 * tq), 0) + tq
    ki = lax.broadcasted_iota(jnp.int32, (tq, 2 * tq), 1)
    dist = qi - ki
    mask = (dist >= 0) & (dist <= tq) & ((n > 0) | (ki >= tq))
    q = q_ref[...].astype(BF16)
    k = jnp.concatenate([kp_ref[...], kc_ref[...]], axis=0).astype(BF16)
    v = jnp.concatenate([vp_ref[...], vc_ref[...]], axis=0).astype(BF16)
    outs, lses = [], []
    for h in range(HEADS):
        sl = slice(h * HEAD_DIM, (h + 1) * HEAD_DIM)
        s = jnp.where(mask, _dot_nt(q[:, sl], k[:, sl]), NEG_INF)
        m = jnp.max(s, axis=1, keepdims=True)
        p = jnp.exp(s - m)
        l = jnp.sum(p, axis=1, keepdims=True)
        outs.append(_dot(p.astype(BF16), v[:, sl]) / l)
        lses.append(jnp.broadcast_to(m + jnp.log(l), (tq, HEAD_DIM)))
    o = jnp.concatenate(outs, axis=1)
    lse = jnp.concatenate(lses, axis=1)
    slabs = [slice(t * LANES, (t + 1) * LANES) for t in range(GROUP_WIDTH // LANES)]
    if dil == 1:
        for t, sl in enumerate(slabs):
            o_ref[t] = o[:, sl]
            l_ref[t] = lse[:, sl]
    else:
        for cc in range(dil):
            @pl.when(c == cc)
            def _(cc=cc):
                for t, sl in enumerate(slabs):
                    o_ref[t, pl.ds(cc, tq, stride=dil), :] = o[:, sl]
                    l_ref[t, pl.ds(cc, tq, stride=dil), :] = lse[:, sl]


def _band_attn(qkv_g, g):
    win, dil = GROUPS[g]
    assert win // dil == BAND
    batch, _, lc, _ = qkv_g.shape
    assert lc % BAND == 0
    nblk = lc // BAND
    tiles = GROUP_WIDTH // LANES
    prev = lambda n: jnp.maximum(n - 1, 0)
    blk = (None, None, BAND, GROUP_WIDTH)
    out = pl.BlockSpec((tiles, BAND * dil, LANES), lambda b, n, c: (0, b * nblk + n, 0))
    return pl.pallas_call(
        functools.partial(_band_attn_kernel, dil=dil),
        grid=(batch, nblk, dil),
        in_specs=[pl.BlockSpec(blk, lambda b, n, c: (b, c, n, 0)),
                  pl.BlockSpec(blk, lambda b, n, c: (b, c, prev(n), 1)),
                  pl.BlockSpec(blk, lambda b, n, c: (b, c, n, 1)),
                  pl.BlockSpec(blk, lambda b, n, c: (b, c, prev(n), 2)),
                  pl.BlockSpec(blk, lambda b, n, c: (b, c, n, 2))],
        out_specs=[out, out],
        out_shape=[jax.ShapeDtypeStruct((tiles, batch * lc * dil, LANES), F32)] * 2,
        compiler_params=_params("parallel", "arbitrary", "arbitrary"),
        name=f"band_attn_g{g}",
    )(qkv_g, qkv_g, qkv_g, qkv_g, qkv_g)


def _sample_attn_kernel(q_ref, kn_ref, vn_ref, k_ref, v_ref, *refs, dil, steps):
    ko_ref, vo_ref, o_ref, l_ref = refs[-4:]
    win = k_ref.shape[-1]
    last = win - LANES
    lane_w = lax.broadcasted_iota(jnp.int32, (steps, win), 1)
    lane_f = lax.broadcasted_iota(jnp.int32, (steps, LANES), 1)
    step_w = lax.broadcasted_iota(jnp.int32, (steps, win), 0)
    step_f = lax.broadcasted_iota(jnp.int32, (steps, LANES), 0)
    row_w = lane_w + steps
    mask_w = (((row_w - step_w) & (dil - 1)) == 0) & (row_w <= win + step_w)
    mask_f = (lane_f < steps) & (lane_f >= step_f) & (((lane_f - step_f) & (dil - 1)) == 0)
    new_lane = lax.broadcasted_iota(jnp.int32, (HEAD_DIM, LANES), 1) >= LANES - steps

    def shifted(old, new):
        moved = pltpu.roll(old, win - steps, 1)
        tail = jnp.where(new_lane, new, moved[:, last:])
        return tail if last == 0 else jnp.concatenate([moved[:, :last], tail], axis=1)

    def body(r, carry):
        k_old, v_old = k_ref[r], v_ref[r]
        k_new = shifted(k_old, kn_ref[r])
        v_new = shifted(v_old, vn_ref[r])
        ko_ref[r] = k_new
        vo_ref[r] = v_new
        q = q_ref[r].astype(BF16)
        s_w = jnp.where(mask_w, _dot(q, k_new.astype(BF16)), NEG_INF)
        s_f = jnp.where(mask_f, _dot(q, k_old[:, :LANES].astype(BF16)), NEG_INF)
        m = jnp.maximum(jnp.max(s_w, axis=1, keepdims=True), jnp.max(s_f, axis=1, keepdims=True))
        p_w = jnp.exp(s_w - m)
        p_f = jnp.exp(s_f - m)
        l = jnp.sum(p_w, axis=1, keepdims=True) + jnp.sum(p_f, axis=1, keepdims=True)
        o = (_dot_nt(p_w.astype(BF16), v_new.astype(BF16)) +
             _dot_nt(p_f.astype(BF16), v_old[:, :LANES].astype(BF16)))
        o_ref[r] = o / l
        l_ref[r] = jnp.broadcast_to(m + jnp.log(l), (steps, HEAD_DIM))
        return carry
    lax.fori_loop(0, k_ref.shape[0], body, 0)


def _sample_attn(q, k_new_t, v_new_t, cache_kt, cache_vt, prev_out, a, g):
    win, dil = GROUPS[g]
    assert win // dil == BAND
    layers, rows, _, w = cache_kt.shape
    assert w == win
    steps = q.shape[1]
    rb = max(1, min(rows, CACHE_BLOCK_BYTES // (HEAD_DIM * win * 4)))
    cache = lambda: pl.BlockSpec((None, rb, HEAD_DIM, win), lambda i: (a, i, 0, 0))
    tok = lambda: pl.BlockSpec((rb, steps, HEAD_DIM), lambda i: (i, 0, 0))
    new = lambda: pl.BlockSpec((rb, HEAD_DIM, LANES), lambda i: (i, 0, 0))
    in_specs = [tok(), new(), new(), cache(), cache()]
    args = [q, k_new_t, v_new_t, cache_kt, cache_vt]
    aliases = {}
    if prev_out is not None:
        in_specs += [pl.BlockSpec(memory_space=pl.ANY)] * 2
        args += list(prev_out)
        aliases = {5: 0, 6: 1}
    return pl.pallas_call(
        functools.partial(_sample_attn_kernel, dil=dil, steps=steps),
        grid=(rows // rb,),
        in_specs=in_specs,
        out_specs=[cache(), cache(), tok(), tok()],
        out_shape=[jax.ShapeDtypeStruct(cache_kt.shape, F32)] * 2 +
                  [jax.ShapeDtypeStruct(q.shape, F32)] * 2,
        input_output_aliases=aliases,
        compiler_params=_params("parallel"),
        name=f"sample_attn_g{g}",
    )(*args)


def _merge_kernel(*refs, alpha):
    o_refs = refs[0:N_GROUPS]
    l_refs = refs[N_GROUPS:2 * N_GROUPS]
    x_ref, wo_ref, lng_ref, lnb_ref, out_ref, outb_ref = refs[2 * N_GROUPS:]
    lses = [r[...] for r in l_refs]
    top = functools.reduce(jnp.maximum, lses)
    es = [jnp.exp(l - top) for l in lses]
    den = functools.reduce(jnp.add, es)
    y = functools.reduce(jnp.add, [(e / den) * r[...] for e, r in zip(es, o_refs)])
    y = jnp.concatenate([y[t] for t in range(y.shape[0])], axis=1)
    m = _dot(y.astype(BF16), wo_ref[0])
    out = _layer_norm(alpha * x_ref[...] + m, lng_ref[0, 0], lnb_ref[0, 0])
    out_ref[...] = out
    outb_ref[...] = out.astype(BF16)


def _merge(outs, lses, x, w_o, ln_g, ln_b, a, li, alpha):
    t, d = x.shape
    tm = min(512, t)
    grp = lambda: pl.BlockSpec((GROUP_WIDTH // LANES, tm, LANES), lambda i: (0, i, 0))
    tok = lambda: pl.BlockSpec((tm, d), lambda i: (i, 0))
    return pl.pallas_call(
        functools.partial(_merge_kernel, alpha=alpha),
        grid=(t // tm,),
        in_specs=[grp() for _ in range(2 * N_GROUPS)] + [
            tok(), pl.BlockSpec((1, GROUP_WIDTH, d), lambda i: (a, 0, 0)),
            _ln_spec(d, li, 0, 1), _ln_spec(d, li, 0, 1)],
        out_specs=[tok(), tok()],
        out_shape=[jax.ShapeDtypeStruct((t, d), F32), jax.ShapeDtypeStruct((t, d), BF16)],
        compiler_params=_params("parallel"),
        name="merge_wo",
    )(*outs, *lses, x, w_o, ln_g, ln_b)


def _heads_first(x, nbatch, steps):
    x = x.reshape(steps, nbatch, HEADS, HEAD_DIM)
    return jnp.transpose(x, (1, 2, 0, 3)).reshape(nbatch * HEADS, steps, HEAD_DIM)


def _new_columns(x, nbatch, steps):
    x = x.reshape(steps, nbatch, HEADS, HEAD_DIM)
    x = jnp.transpose(x, (1, 2, 3, 0)).reshape(nbatch * HEADS, HEAD_DIM, steps)
    return jnp.pad(x, ((0, 0), (0, 0), (LANES - steps, 0)))


def _tokens_first(x, nbatch, steps):
    x = x.reshape(nbatch, HEADS // 2, 2, steps, HEAD_DIM)
    return jnp.transpose(x, (1, 3, 0, 2, 4)).reshape(HEADS // 2, steps * nbatch, LANES)


def kernel(x_prompt, x_sample, state_conv, cache_k_w128, cache_v_w128, cache_k_w512, cache_v_w512, cache_k_w2048, cache_v_w2048, ln_g, ln_b, w_pw1, b_pw1, w_dw, b_dw, cn_g, cn_b, w_pw2, b_pw2, w_qkv, w_o, w_ffn_gu, w_ffn_down, w_router, b_router, w_exp_gu, w_exp_down):
    batch, seq, d = x_prompt.shape
    nbatch, s_new, _ = x_sample.shape
    depth = ln_g.shape[0]
    alpha = (2 * depth) ** 0.25
    caches = ((cache_k_w128, cache_v_w128), (cache_k_w512, cache_v_w512),
              (cache_k_w2048, cache_v_w2048))
    caches_t = [[jnp.transpose(c, (0, 1, 3, 4, 2)).reshape(c.shape[0], nbatch * HEADS, HEAD_DIM, c.shape[2])
                 for c in kv] for kv in caches]

    w_pw1_h, w_pw2_h, w_qkv_h, w_o_h = (w.astype(BF16) for w in (w_pw1, w_pw2, w_qkv, w_o))
    w_ffn_gu_h, w_ffn_down_h = w_ffn_gu.astype(BF16), w_ffn_down.astype(BF16)
    w_exp_gu_h, w_exp_down_h = w_exp_gu.astype(BF16), w_exp_down.astype(BF16)
    row = lambda v: v[:, None, :]
    conv_wts = (w_dw, row(b_dw), row(cn_g), row(cn_b), w_pw2_h, row(b_pw2))
    ln_g4, ln_b4 = ln_g[:, :, None, :], ln_b[:, :, None, :]
    w_router_t = jnp.swapaxes(w_router, 1, 2)
    b_router3 = row(b_router)
    state_t = jnp.transpose(state_conv, (0, 2, 1, 3))

    pos_p = jnp.tile(jnp.arange(seq), batch)
    pos_s = jnp.repeat(PAST_LEN + jnp.arange(s_new), nbatch)
    dils = tuple(dil for _, dil in GROUPS)

    xp = x_prompt.reshape(batch * seq, d)
    xs = jnp.transpose(x_sample, (1, 0, 2)).reshape(s_new * nbatch, d)
    conv_p, conv_s = [], []
    kv_p = [[] for _ in range(2 * N_GROUPS)]
    kv_s = [None] * N_GROUPS
    for li in range(depth):
        a = li // 2
        if li % 2 == 0:
            wts = conv_wts + (ln_g4, ln_b4)
            up = _pw1_glu(xp, w_pw1_h, row(b_pw1), a)
            us = _pw1_glu(xs, w_pw1_h, row(b_pw1), a)
            xp = _conv_prompt(up, xp, batch, wts, a, li, alpha)
            xs, st = _conv_sample(us, xs, state_t, wts, a, li, alpha)
            conv_p.append(up.reshape(batch, seq, d)[:, seq - CONV_HIST:])
            conv_s.append(st)
            xp = _ffn(xp, w_ffn_gu_h, w_ffn_down_h, ln_g4, ln_b4, a, li, alpha)
            xs = _ffn(xs, w_ffn_gu_h, w_ffn_down_h, ln_g4, ln_b4, a, li, alpha)
        else:
            qkv_p = _qkv(xp, w_qkv_h, pos_p, a, batch, dils)
            qkv_s = _qkv(xs, w_qkv_h, pos_s, a, 1, (1,) * N_GROUPS)
            outs_p, lses_p, outs_s, lses_s = [], [], [], []
            for g, (win, dil) in enumerate(GROUPS):
                o, l = _band_attn(qkv_p[g], g)
                outs_p.append(o)
                lses_p.append(l)
                lc = seq // dil
                for j in range(2):
                    tail = qkv_p[g][:, :, lc - BAND:, (1 + j) * GROUP_WIDTH:(2 + j) * GROUP_WIDTH]
                    tail = jnp.swapaxes(tail, 1, 2).reshape(batch, win, HEADS, HEAD_DIM)
                    kv_p[2 * g + j].append(tail)
                part = lambda j: qkv_s[g][0, 0, :, j * GROUP_WIDTH:(j + 1) * GROUP_WIDTH]
                res = _sample_attn(_heads_first(part(0), nbatch, s_new),
                                   _new_columns(part(1), nbatch, s_new),
                                   _new_columns(part(2), nbatch, s_new),
                                   caches_t[g][0], caches_t[g][1], kv_s[g], a, g)
                kv_s[g] = res[:2]
                outs_s.append(_tokens_first(res[2], nbatch, s_new))
                lses_s.append(_tokens_first(res[3], nbatch, s_new))
            xp, xpb = _merge(outs_p, lses_p, xp, w_o_h, ln_g4, ln_b4, a, li, alpha)
            xs, xsb = _merge(outs_s, lses_s, xs, w_o_h, ln_g4, ln_b4, a, li, alpha)
            for stream in range(2):
                x, xb = (xp, xpb) if stream == 0 else (xs, xsb)
                combine, rank = _router(x, w_router_t, b_router3, a, min(MOE_TOKEN_TILE, x.shape[0]))
                y = _moe(xb, combine, rank, w_exp_gu_h, w_exp_down_h, a)
                x = _residual_ln(x, y, ln_g4, ln_b4, li, alpha)
                if stream == 0:
                    xp = x
                else:
                    xs = x

    kv_p_out = [jnp.stack(t) for t in kv_p]
    kv_s_out = []
    for g, (win, _) in enumerate(GROUPS):
        for j in range(2):
            ct = kv_s[g][j].reshape(-1, nbatch, HEADS, HEAD_DIM, win)
            kv_s_out.append(jnp.transpose(ct, (0, 1, 4, 2, 3)))
    conv_s_out = jnp.transpose(jnp.concatenate(conv_s, axis=0), (0, 2, 1, 3))
    y_sample = jnp.transpose(xs.reshape(s_new, nbatch, d), (1, 0, 2))
    return (xp.reshape(batch, seq, d), y_sample, jnp.stack(conv_p), conv_s_out,
            *kv_p_out, *kv_s_out)
```

```python
import functools

import jax
import jax.numpy as jnp
from jax import lax
from jax.experimental import pallas as pl
from jax.experimental.pallas import tpu as pltpu

GROUPS = ((128, 1), (512, 4), (2048, 16))
N_GROUPS = len(GROUPS)
HEADS = 8
HEAD_DIM = 64
GROUP_WIDTH = HEADS * HEAD_DIM
BAND = 128
CONV_WIDTH = 31
CONV_HIST = CONV_WIDTH - 1
CONV_HALO = 32
PAST_LEN = 2048
ROPE_THETA = 10000.0
LN_EPS = 1e-5
NEG_INF = -1e30
LANES = 128

VMEM_LIMIT_BYTES = 56 * 1024 * 1024
TOKEN_TILE = 1024
FF_TILE = 512
MOE_TOKEN_TILE = 2048
MOE_FF_TILE = 896
MOE_CHUNK = 256
MOE_SLOTS = 128
SLOT_SHIFT = MOE_SLOTS.bit_length() - 1
ROUTER_TILE = 512
CACHE_BLOCK_BYTES = 2 * 1024 * 1024
SAMPLE_ROW_GROUP = 4

F32 = jnp.float32
BF16 = jnp.bfloat16


def _params(*semantics):
    return pltpu.CompilerParams(dimension_semantics=semantics,
                                vmem_limit_bytes=VMEM_LIMIT_BYTES)


def _layer_norm(x, g, b):
    mu = jnp.mean(x, axis=-1, keepdims=True)
    xc = x - mu
    var = jnp.mean(xc * xc, axis=-1, keepdims=True)
    return xc * lax.rsqrt(var + LN_EPS) * g + b


def _sigmoid(x):
    return 1.0 / (1.0 + jnp.exp(-x))


def _dot(a, b):
    return jnp.dot(a, b, preferred_element_type=F32)


def _dot_nt(a, b):
    return lax.dot_general(a, b, (((1,), (1,)), ((), ())), preferred_element_type=F32)


def _token_tile(t):
    return min(TOKEN_TILE, t)


def _ln_spec(d, li, which, nargs):
    idx = (li, which, 0, 0)
    return pl.BlockSpec((1, 1, 1, d), {1: lambda i: idx, 2: lambda i, j: idx,
                                       3: lambda i, j, k: idx}[nargs])


def _pw1_glu_kernel(x_ref, w_ref, b_ref, u_ref):
    d = u_ref.shape[-1]
    x = x_ref[...].astype(BF16)
    a = _dot(x, w_ref[0, :, :d]) + b_ref[0, :, :d]
    g = _dot(x, w_ref[0, :, d:]) + b_ref[0, :, d:]
    u_ref[...] = a * _sigmoid(g)


def _pw1_glu(x, w_pw1, b_pw1, a):
    t, d = x.shape
    tm = _token_tile(t)
    return pl.pallas_call(
        _pw1_glu_kernel,
        grid=(t // tm,),
        in_specs=[pl.BlockSpec((tm, d), lambda i: (i, 0)),
                  pl.BlockSpec((1, d, 2 * d), lambda i: (a, 0, 0)),
                  pl.BlockSpec((1, 1, 2 * d), lambda i: (a, 0, 0))],
        out_specs=pl.BlockSpec((tm, d), lambda i: (i, 0)),
        out_shape=jax.ShapeDtypeStruct((t, d), F32),
        compiler_params=_params("parallel"),
        name="pw1_glu",
    )(x, w_pw1, b_pw1)


def _conv_tail(y, x, bdw_ref, cng_ref, cnb_ref, w2_ref, b2_ref, lng_ref, lnb_ref, alpha):
    y = y + bdw_ref[0]
    y = _layer_norm(y, cng_ref[0], cnb_ref[0])
    y = y * _sigmoid(y)
    m = _dot(y.astype(BF16), w2_ref[0]) + b2_ref[0]
    return _layer_norm(alpha * x + m, lng_ref[0, 0], lnb_ref[0, 0])


CONV_ROW_CHUNK = 32
CONV_LANE_CHUNK = 256
SUBLANES = 8


def _conv_prompt_kernel(um_ref, uh_ref, x_ref, wdw_ref, bdw_ref, cng_ref, cnb_ref, w2_ref,
                        b2_ref, lng_ref, lnb_ref, o_ref, win_ref, y_ref, *, alpha):
    i = pl.program_id(1)
    tt = um_ref.shape[1]
    win_ref[0:CONV_HALO, :] = jnp.where(i > 0, uh_ref[0], 0.0)
    win_ref[CONV_HALO:CONV_HALO + tt, :] = um_ref[0]
    lead = CONV_HALO - CONV_HIST
    span = 2 * CONV_ROW_CHUNK
    assert lead + CONV_WIDTH - 1 + CONV_ROW_CHUNK <= span
    for lc in range(0, um_ref.shape[2], CONV_LANE_CHUNK):
        lanes = slice(lc, lc + CONV_LANE_CHUNK)
        w = wdw_ref[0, :, lanes]
        for r in range(0, tt, CONV_ROW_CHUNK):
            slab = win_ref[r:r + span, lanes]
            acc = None
            for res in range(SUBLANES):
                rolled = slab if res == 0 else pltpu.roll(slab, span - res, 0)
                for j in range(CONV_WIDTH):
                    if (lead + j) % SUBLANES == res:
                        base = lead + j - res
                        term = w[j:j + 1] * rolled[base:base + CONV_ROW_CHUNK]
                        acc = term if acc is None else acc + term
            y_ref[r:r + CONV_ROW_CHUNK, lanes] = acc
    o_ref[...] = _conv_tail(y_ref[...], x_ref[...], bdw_ref, cng_ref, cnb_ref, w2_ref, b2_ref,
                            lng_ref, lnb_ref, alpha)


def _conv_prompt(u, x, batch, wts, a, li, alpha):
    w_dw, b_dw, cn_g, cn_b, w_pw2, b_pw2, ln_g, ln_b = wts
    t, d = x.shape
    seq = t // batch
    tt = min(512, seq)
    nt = seq // tt
    hb = tt // CONV_HALO
    u3 = u.reshape(batch, seq, d)
    vec = lambda: pl.BlockSpec((1, 1, d), lambda b, i: (a, 0, 0))
    return pl.pallas_call(
        functools.partial(_conv_prompt_kernel, alpha=alpha),
        grid=(batch, nt),
        in_specs=[pl.BlockSpec((1, tt, d), lambda b, i: (b, i, 0)),
                  pl.BlockSpec((1, CONV_HALO, d), lambda b, i: (b, jnp.maximum(i * hb - 1, 0), 0)),
                  pl.BlockSpec((tt, d), lambda b, i: (b * nt + i, 0)),
                  pl.BlockSpec((1, CONV_WIDTH, d), lambda b, i: (a, 0, 0)),
                  vec(), vec(), vec(),
                  pl.BlockSpec((1, d, d), lambda b, i: (a, 0, 0)),
                  vec(),
                  _ln_spec(d, li, 0, 2), _ln_spec(d, li, 0, 2)],
        out_specs=pl.BlockSpec((tt, d), lambda b, i: (b * nt + i, 0)),
        out_shape=jax.ShapeDtypeStruct((t, d), F32),
        scratch_shapes=[pltpu.VMEM((CONV_HALO + tt, d), F32), pltpu.VMEM((tt, d), F32)],
        compiler_params=_params("parallel", "arbitrary"),
        name="conv_prompt",
    )(u3, u3, x, w_dw, b_dw, cn_g, cn_b, w_pw2, b_pw2, ln_g, ln_b)


def _conv_sample_kernel(u_ref, st_ref, x_ref, wdw_ref, bdw_ref, cng_ref, cnb_ref, w2_ref,
                        b2_ref, lng_ref, lnb_ref, o_ref, ns_ref, win_ref, *, alpha):
    s, nb, d = u_ref.shape
    win_ref[0:CONV_HIST] = st_ref[0]
    win_ref[CONV_HIST:CONV_HIST + s] = u_ref[...]
    ns_ref[0] = win_ref[s:s + CONV_HIST]
    acc = wdw_ref[0, 0:1, :] * win_ref[0:s]
    for j in range(1, CONV_WIDTH):
        acc = acc + wdw_ref[0, j:j + 1, :] * win_ref[j:j + s]
    out = _conv_tail(acc.reshape(s * nb, d), x_ref[...].reshape(s * nb, d), bdw_ref, cng_ref,
                     cnb_ref, w2_ref, b2_ref, lng_ref, lnb_ref, alpha)
    o_ref[...] = out.reshape(s, nb, d)


def _conv_sample(u, x, state_t, wts, a, li, alpha):
    w_dw, b_dw, cn_g, cn_b, w_pw2, b_pw2, ln_g, ln_b = wts
    t, d = x.shape
    nbatch = state_t.shape[2]
    s = t // nbatch
    nb = min(16, nbatch)
    vec = lambda: pl.BlockSpec((1, 1, d), lambda i: (a, 0, 0))
    tok = lambda: pl.BlockSpec((s, nb, d), lambda i: (0, i, 0))
    x_new, st_new = pl.pallas_call(
        functools.partial(_conv_sample_kernel, alpha=alpha),
        grid=(nbatch // nb,),
        in_specs=[tok(),
                  pl.BlockSpec((1, CONV_HIST, nb, d), lambda i: (a, 0, i, 0)),
                  tok(),
                  pl.BlockSpec((1, CONV_WIDTH, d), lambda i: (a, 0, 0)),
                  vec(), vec(), vec(),
                  pl.BlockSpec((1, d, d), lambda i: (a, 0, 0)),
                  vec(),
                  _ln_spec(d, li, 0, 1), _ln_spec(d, li, 0, 1)],
        out_specs=[tok(), pl.BlockSpec((1, CONV_HIST, nb, d), lambda i: (0, 0, i, 0))],
        out_shape=[jax.ShapeDtypeStruct((s, nbatch, d), F32),
                   jax.ShapeDtypeStruct((1, CONV_HIST, nbatch, d), F32)],
        scratch_shapes=[pltpu.VMEM((CONV_HIST + s, nb, d), F32)],
        compiler_params=_params("parallel"),
        name="conv_sample",
    )(u.reshape(s, nbatch, d), state_t, x.reshape(s, nbatch, d), w_dw, b_dw, cn_g, cn_b,
      w_pw2, b_pw2, ln_g, ln_b)
    return x_new.reshape(t, d), st_new


def _ffn_kernel(x_ref, wg_ref, wu_ref, wd_ref, lng_ref, lnb_ref, o_ref, acc_ref, *, alpha):
    k = pl.program_id(1)

    @pl.when(k == 0)
    def _():
        acc_ref[...] = jnp.zeros_like(acc_ref)

    x = x_ref[...].astype(BF16)
    g = _dot(x, wg_ref[0])
    u = _dot(x, wu_ref[0])
    h = (g * _sigmoid(g) * u).astype(BF16)
    acc_ref[...] += _dot(h, wd_ref[0])

    @pl.when(k == pl.num_programs(1) - 1)
    def _():
        o_ref[...] = _layer_norm(alpha * x_ref[...] + acc_ref[...], lng_ref[0, 0], lnb_ref[0, 0])


def _ffn(x, w_gu, w_down, ln_g, ln_b, a, li, alpha):
    t, d = x.shape
    f = w_down.shape[1]
    tm = _token_tile(t)
    tf = min(FF_TILE, f)
    nf = f // tf
    return pl.pallas_call(
        functools.partial(_ffn_kernel, alpha=alpha),
        grid=(t // tm, nf),
        in_specs=[pl.BlockSpec((tm, d), lambda i, k: (i, 0)),
                  pl.BlockSpec((1, d, tf), lambda i, k: (a, 0, k)),
                  pl.BlockSpec((1, d, tf), lambda i, k: (a, 0, k + nf)),
                  pl.BlockSpec((1, tf, d), lambda i, k: (a, k, 0)),
                  _ln_spec(d, li, 1, 2), _ln_spec(d, li, 1, 2)],
        out_specs=pl.BlockSpec((tm, d), lambda i, k: (i, 0)),
        out_shape=jax.ShapeDtypeStruct((t, d), F32),
        scratch_shapes=[pltpu.VMEM((tm, d), F32)],
        compiler_params=_params("parallel", "arbitrary"),
        name="ffn_dense",
    )(x, w_gu, w_gu, w_down, ln_g, ln_b)


def _router_kernel(x_ref, wt_ref, b_ref, c_ref, r_ref, count_ref, *, tiles_per_group):
    i = pl.program_id(0)
    x = x_ref[...]
    tr = x.shape[0]
    n_exp = wt_ref.shape[1]
    lane = lax.broadcasted_iota(jnp.int32, (tr, LANES), 1).astype(F32)
    logits = jnp.where(lane < n_exp, b_ref[0], -jnp.inf)
    for e in range(n_exp):
        col = jnp.sum(x * wt_ref[0, e:e + 1, :], axis=1, keepdims=True)
        logits = logits + jnp.where(lane == e, col, 0.0)
    m1 = jnp.max(logits, axis=1, keepdims=True)
    i1 = jnp.min(jnp.where(logits == m1, lane, float(LANES)), axis=1, keepdims=True)
    rest = jnp.where(lane == i1, -jnp.inf, logits)
    m2 = jnp.max(rest, axis=1, keepdims=True)
    i2 = jnp.min(jnp.where(rest == m2, lane, float(LANES)), axis=1, keepdims=True)
    e2 = jnp.exp(m2 - m1)
    g1 = 1.0 / (1.0 + e2)
    g2 = e2 / (1.0 + e2)
    c_ref[...] = jnp.where(lane == i1, g1, 0.0) + jnp.where(lane == i2, g2, 0.0)

    @pl.when(i % tiles_per_group == 0)
    def _():
        count_ref[...] = jnp.zeros_like(count_ref)

    member = (lane == i1) | (lane == i2)
    below = (lax.broadcasted_iota(jnp.int32, (tr, tr), 0) >
             lax.broadcasted_iota(jnp.int32, (tr, tr), 1))
    earlier = _dot(below.astype(BF16), member.astype(BF16))
    r_ref[...] = jnp.where(member, earlier + count_ref[...], -1.0)
    count_ref[...] += jnp.sum(member.astype(F32), axis=0, keepdims=True)


def _router(x, w_router_t, b_router, a, moe_tile):
    t, d = x.shape
    n_exp = w_router_t.shape[1]
    tr = min(ROUTER_TILE, t)
    b_pad = jnp.pad(b_router, ((0, 0), (0, 0), (0, LANES - n_exp)))
    return pl.pallas_call(
        functools.partial(_router_kernel, tiles_per_group=moe_tile // tr),
        grid=(t // tr,),
        in_specs=[pl.BlockSpec((tr, d), lambda i: (i, 0)),
                  pl.BlockSpec((1, n_exp, d), lambda i: (a, 0, 0)),
                  pl.BlockSpec((1, 1, LANES), lambda i: (a, 0, 0))],
        out_specs=[pl.BlockSpec((tr, LANES), lambda i: (i, 0)),
                   pl.BlockSpec((tr, LANES), lambda i: (i, 0))],
        out_shape=[jax.ShapeDtypeStruct((t, LANES), F32)] * 2,
        scratch_shapes=[pltpu.VMEM((1, LANES), F32)],
        compiler_params=_params("arbitrary"),
        name="router",
    )(x, w_router_t, b_pad)


def _slot_rows(q):
    return pl.ds(pl.multiple_of(q * MOE_SLOTS, MOE_SLOTS), MOE_SLOTS)


def _moe_kernel(pre_ref, xb_ref, rt_ref, r_ref, c_ref, wg_ref, wu_ref, wd_ref, y_ref,
                xs_ref, ys_ref, *, n_chunks, chunk):
    i, e, k = pl.program_id(0), pl.program_id(1), pl.program_id(2)
    base = (i * pl.num_programs(1) + e) * (n_chunks + 1)
    n_sel = pre_ref[base + n_chunks]
    n_blocks = (n_sel + (MOE_SLOTS - 1)) >> SLOT_SHIFT

    def chunk_blocks(c):
        s0, s1 = pre_ref[base + c], pre_ref[base + c + 1]
        lo = s0 >> SLOT_SHIFT
        hi = (s1 + (MOE_SLOTS - 1)) >> SLOT_SHIFT
        return lo, jnp.where(s1 > s0, hi, lo)

    @pl.when((e == 0) & (k == 0))
    def _():
        y_ref[...] = jnp.zeros_like(y_ref)

    @pl.when(k == 0)
    def _():
        def clear(q, carry):
            xs_ref[_slot_rows(q), :] = jnp.zeros((MOE_SLOTS, xs_ref.shape[1]), BF16)
            return carry
        lax.fori_loop(0, n_blocks, clear, 0)
        for c in range(n_chunks):
            rank_row = rt_ref[0, :, c * chunk:(c + 1) * chunk]
            xc = xb_ref[c * chunk:(c + 1) * chunk, :]
            lo, hi = chunk_blocks(c)

            def gather(q, carry, rank_row=rank_row, xc=xc):
                slot = q * MOE_SLOTS + lax.broadcasted_iota(jnp.int32, (MOE_SLOTS, chunk), 0)
                onehot = (slot.astype(F32) == rank_row).astype(BF16)
                xs_ref[_slot_rows(q), :] += _dot(onehot, xc).astype(BF16)
                return carry
            lax.fori_loop(lo, hi, gather, 0)

    def ffn(start, size):
        rows = pl.ds(pl.multiple_of(start, MOE_SLOTS), size)
        xq = xs_ref[rows, :]
        g = _dot(xq, wg_ref[0, 0])
        u = _dot(xq, wu_ref[0, 0])
        h = (g * _sigmoid(g) * u).astype(BF16)
        part = _dot(h, wd_ref[0, 0])

        @pl.when(k == 0)
        def _():
            ys_ref[rows, :] = part

        @pl.when(k > 0)
        def _():
            ys_ref[rows, :] += part

    def ffn_pair(p, carry):
        ffn(p * (2 * MOE_SLOTS), 2 * MOE_SLOTS)
        return carry
    lax.fori_loop(0, n_blocks >> 1, ffn_pair, 0)

    @pl.when((n_blocks & 1) == 1)
    def _():
        ffn((n_blocks - 1) * MOE_SLOTS, MOE_SLOTS)

    @pl.when(k == pl.num_programs(2) - 1)
    def _():
        for c in range(n_chunks):
            tok = slice(c * chunk, (c + 1) * chunk)
            lane = lax.broadcasted_iota(jnp.int32, (chunk, LANES), 1)
            rank_col = jnp.sum(jnp.where(lane == e, r_ref[tok, :], 0.0), axis=1, keepdims=True)
            gate_col = jnp.sum(jnp.where(lane == e, c_ref[tok, :], 0.0), axis=1, keepdims=True)
            lo, hi = chunk_blocks(c)

            def scatter(q, carry, tok=tok, rank_col=rank_col, gate_col=gate_col):
                slot = q * MOE_SLOTS + lax.broadcasted_iota(jnp.int32, (chunk, MOE_SLOTS), 1)
                onehot = (rank_col == slot.astype(F32)).astype(BF16)
                y_ref[tok, :] += gate_col * _dot(onehot, ys_ref[_slot_rows(q), :].astype(BF16))
                return carry
            lax.fori_loop(lo, hi, scatter, 0)


def _moe(xb, combine, rank, w_gu, w_down, a):
    t, d = xb.shape
    n_exp, f = w_down.shape[1], w_down.shape[2]
    tm = min(MOE_TOKEN_TILE, t)
    nt = t // tm
    chunk = min(MOE_CHUNK, tm)
    n_chunks = tm // chunk
    cap = -(-tm // MOE_SLOTS) * MOE_SLOTS
    tf = MOE_FF_TILE if f % MOE_FF_TILE == 0 else min(FF_TILE, f)
    nf = f // tf
    member = (rank[:, :n_exp] >= 0).astype(jnp.int32)
    per_chunk = member.reshape(nt, n_chunks, chunk, n_exp).sum(axis=2)
    pre = jnp.concatenate([jnp.zeros((nt, 1, n_exp), jnp.int32), jnp.cumsum(per_chunk, axis=1)], axis=1)
    pre = jnp.swapaxes(pre, 1, 2).reshape(-1)
    rank_t = jnp.swapaxes(rank[:, :n_exp], 0, 1)[:, None, :]
    grid_spec = pltpu.PrefetchScalarGridSpec(
        num_scalar_prefetch=1,
        grid=(nt, n_exp, nf),
        in_specs=[pl.BlockSpec((tm, d), lambda i, e, k, pre: (i, 0)),
                  pl.BlockSpec((1, 1, tm), lambda i, e, k, pre: (e, 0, i)),
                  pl.BlockSpec((tm, LANES), lambda i, e, k, pre: (i, 0)),
                  pl.BlockSpec((tm, LANES), lambda i, e, k, pre: (i, 0)),
                  pl.BlockSpec((1, 1, d, tf), lambda i, e, k, pre: (a, e, 0, k)),
                  pl.BlockSpec((1, 1, d, tf), lambda i, e, k, pre: (a, e, 0, k + nf)),
                  pl.BlockSpec((1, 1, tf, d), lambda i, e, k, pre: (a, e, k, 0))],
        out_specs=pl.BlockSpec((tm, d), lambda i, e, k, pre: (i, 0)),
        scratch_shapes=[pltpu.VMEM((cap, d), BF16), pltpu.VMEM((cap, d), F32)])
    return pl.pallas_call(
        functools.partial(_moe_kernel, n_chunks=n_chunks, chunk=chunk),
        grid_spec=grid_spec,
        out_shape=jax.ShapeDtypeStruct((t, d), F32),
        compiler_params=_params("parallel", "arbitrary", "arbitrary"),
        name="moe_sparse",
    )(pre, xb, rank_t, rank, combine, w_gu, w_gu, w_down)


def _residual_ln_kernel(x_ref, y_ref, lng_ref, lnb_ref, o_ref, *, alpha):
    o_ref[...] = _layer_norm(alpha * x_ref[...] + y_ref[...], lng_ref[0, 0], lnb_ref[0, 0])


def _residual_ln(x, y, ln_g, ln_b, li, alpha):
    t, d = x.shape
    tm = _token_tile(t)
    tok = lambda: pl.BlockSpec((tm, d), lambda i: (i, 0))
    return pl.pallas_call(
        functools.partial(_residual_ln_kernel, alpha=alpha),
        grid=(t // tm,),
        in_specs=[tok(), tok(), _ln_spec(d, li, 1, 1), _ln_spec(d, li, 1, 1)],
        out_specs=tok(),
        out_shape=jax.ShapeDtypeStruct((t, d), F32),
        compiler_params=_params("parallel"),
        name="residual_ln",
    )(x, y, ln_g, ln_b)


def _qkv_kernel(x_ref, w_ref, cos_ref, sin_ref, *refs, dils):
    o_refs, z_ref, s_ref = refs[:N_GROUPS], refs[N_GROUPS], refs[N_GROUPS + 1]
    j = pl.program_id(1)
    g = j // 3
    kind = j - 3 * g
    z = _dot(x_ref[...].astype(BF16), w_ref[0])
    tm, width = z.shape

    @pl.when(kind < 2)
    def _():
        cos, sin = cos_ref[...], sin_ref[...]
        lane = lax.broadcasted_iota(jnp.int32, cos.shape, 1)
        half = HEAD_DIM // 2
        first = (lane & (HEAD_DIM - 1)) < half
        scale = jnp.where(kind == 0, HEAD_DIM ** -0.5, 1.0)
        for t in range(width // LANES):
            zt = z[:, t * LANES:(t + 1) * LANES]
            partner = jnp.where(first, pltpu.roll(zt, LANES - half, 1), pltpu.roll(zt, half, 1))
            z_ref[:, t * LANES:(t + 1) * LANES] = (zt * cos + partner * sin) * scale

    @pl.when(kind == 2)
    def _():
        z_ref[...] = z

    for gi, dil in enumerate(dils):
        @pl.when(g == gi)
        def _(gi=gi, dil=dil):
            if dil == 1:
                o_refs[gi][0] = z_ref[...]
            else:
                for t in range(width // LANES):
                    s_ref[t] = z_ref[:, t * LANES:(t + 1) * LANES]
                for c in range(dil):
                    for t in range(width // LANES):
                        o_refs[gi][c, :, t * LANES:(t + 1) * LANES] = (
                            s_ref[t, pl.ds(c, tm // dil, stride=dil), :])


def _rope_tables(pos):
    half = HEAD_DIM // 2
    inv = ROPE_THETA ** (-jnp.arange(half, dtype=F32) / half)
    ang = pos.astype(F32)[:, None] * inv[None, :]
    cos, sin = jnp.cos(ang), jnp.sin(ang)
    cos2 = jnp.concatenate([cos, cos], axis=1)
    sin2 = jnp.concatenate([-sin, sin], axis=1)
    return jnp.concatenate([cos2, cos2], axis=1), jnp.concatenate([sin2, sin2], axis=1)


def _qkv(x, w_qkv, pos, a, nseq, dils):
    t, d = x.shape
    seq = t // nseq
    tm = min(512, seq)
    tiles = seq // tm
    cos, sin = _rope_tables(pos)

    def out_spec(gi, dil):
        return pl.BlockSpec(
            (None, dil, tm // dil, GROUP_WIDTH),
            lambda i, j: (i // tiles, 0, i % tiles, jnp.clip(j - 3 * gi, 0, 2)))

    def w_col(i, j):
        return (a, 0, (j % 3) * N_GROUPS + j // 3)

    return pl.pallas_call(
        functools.partial(_qkv_kernel, dils=dils),
        grid=(t // tm, 3 * N_GROUPS),
        in_specs=[pl.BlockSpec((tm, d), lambda i, j: (i, 0)),
                  pl.BlockSpec((1, d, GROUP_WIDTH), w_col),
                  pl.BlockSpec((tm, 2 * HEAD_DIM), lambda i, j: (i, 0)),
                  pl.BlockSpec((tm, 2 * HEAD_DIM), lambda i, j: (i, 0))],
        out_specs=[out_spec(gi, dil) for gi, dil in enumerate(dils)],
        out_shape=[jax.ShapeDtypeStruct((nseq, dil, seq // dil, 3 * GROUP_WIDTH), F32)
                   for dil in dils],
        scratch_shapes=[pltpu.VMEM((tm, GROUP_WIDTH), F32),
                        pltpu.VMEM((GROUP_WIDTH // LANES, tm, LANES), F32)],
        compiler_params=_params("parallel", "arbitrary"),
        name="qkv_rope",
    )(x, w_qkv, cos, sin)


def _band_attn_kernel(q_ref, kp_ref, kc_ref, vp_ref, vc_ref, o_ref, l_ref, *, dil):
    n, c = pl.program_id(1), pl.program_id(2)
    tq = q_ref.shape[0]
    qi = lax.broadcasted_iota(jnp.int32, (tq, 2 * tq), 0) + tq
    ki = lax.broadcasted_iota(jnp.int32, (tq, 2 * tq), 1)
    dist = qi - ki
    mask = (dist >= 0) & (dist <= tq) & ((n > 0) | (ki >= tq))
    q = q_ref[...].astype(BF16)
    k = jnp.concatenate([kp_ref[...], kc_ref[...]], axis=0).astype(BF16)
    v = jnp.concatenate([vp_ref[...], vc_ref[...]], axis=0).astype(BF16)
    outs, lses = [], []
    for h in range(HEADS):
        sl = slice(h * HEAD_DIM, (h + 1) * HEAD_DIM)
        s = jnp.where(mask, _dot_nt(q[:, sl], k[:, sl]), NEG_INF)
        m = jnp.max(s, axis=1, keepdims=True)
        p = jnp.exp(s - m)
        l = jnp.sum(p, axis=1, keepdims=True)
        outs.append(_dot(p.astype(BF16), v[:, sl]) / l)
        lses.append(jnp.broadcast_to(m + jnp.log(l), (tq, HEAD_DIM)))
    o = jnp.concatenate(outs, axis=1)
    lse = jnp.concatenate(lses, axis=1)
    slabs = [slice(t * LANES, (t + 1) * LANES) for t in range(GROUP_WIDTH // LANES)]
    if dil == 1:
        for t, sl in enumerate(slabs):
            o_ref[t] = o[:, sl]
            l_ref[t] = lse[:, sl]
    else:
        for cc in range(dil):
            @pl.when(c == cc)
            def _(cc=cc):
                for t, sl in enumerate(slabs):
                    o_ref[t, pl.ds(cc, tq, stride=dil), :] = o[:, sl]
                    l_ref[t, pl.ds(cc, tq, stride=dil), :] = lse[:, sl]


def _band_attn(qkv_g, g):
    win, dil = GROUPS[g]
    assert win // dil == BAND
    batch, _, lc, _ = qkv_g.shape
    assert lc % BAND == 0
    nblk = lc // BAND
    tiles = GROUP_WIDTH // LANES
    prev = lambda n: jnp.maximum(n - 1, 0)
    blk = (None, None, BAND, GROUP_WIDTH)
    out = pl.BlockSpec((tiles, BAND * dil, LANES), lambda b, n, c: (0, b * nblk + n, 0))
    return pl.pallas_call(
        functools.partial(_band_attn_kernel, dil=dil),
        grid=(batch, nblk, dil),
        in_specs=[pl.BlockSpec(blk, lambda b, n, c: (b, c, n, 0)),
                  pl.BlockSpec(blk, lambda b, n, c: (b, c, prev(n), 1)),
                  pl.BlockSpec(blk, lambda b, n, c: (b, c, n, 1)),
                  pl.BlockSpec(blk, lambda b, n, c: (b, c, prev(n), 2)),
                  pl.BlockSpec(blk, lambda b, n, c: (b, c, n, 2))],
        out_specs=[out, out],
        out_shape=[jax.ShapeDtypeStruct((tiles, batch * lc * dil, LANES), F32)] * 2,
        compiler_params=_params("parallel", "arbitrary", "arbitrary"),
        name=f"band_attn_g{g}",
    )(qkv_g, qkv_g, qkv_g, qkv_g, qkv_g)


def _sample_attn_kernel(q_ref, kn_ref, vn_ref, k_ref, v_ref, *refs, dil, steps):
    ko_ref, vo_ref, o_ref, l_ref = refs[-4:]
    win = k_ref.shape[-1]
    last = win - LANES
    lane_w = lax.broadcasted_iota(jnp.int32, (steps, win), 1)
    lane_f = lax.broadcasted_iota(jnp.int32, (steps, LANES), 1)
    step_w = lax.broadcasted_iota(jnp.int32, (steps, win), 0)
    step_f = lax.broadcasted_iota(jnp.int32, (steps, LANES), 0)
    row_w = lane_w + steps
    mask_w = (((row_w - step_w) & (dil - 1)) == 0) & (row_w <= win + step_w)
    mask_f = (lane_f < steps) & (lane_f >= step_f) & (((lane_f - step_f) & (dil - 1)) == 0)
    new_lane = lax.broadcasted_iota(jnp.int32, (HEAD_DIM, LANES), 1) >= LANES - steps

    def shifted(old, new):
        moved = pltpu.roll(old, win - steps, 1)
        tail = jnp.where(new_lane, new, moved[:, last:])
        return tail if last == 0 else jnp.concatenate([moved[:, :last], tail], axis=1)

    def one_row(r):
        k_old, v_old = k_ref[r], v_ref[r]
        k_new = shifted(k_old, kn_ref[r])
        v_new = shifted(v_old, vn_ref[r])
        ko_ref[r] = k_new
        vo_ref[r] = v_new
        q = q_ref[r].astype(BF16)
        s_w = jnp.where(mask_w, _dot(q, k_new.astype(BF16)), NEG_INF)
        s_f = jnp.where(mask_f, _dot(q, k_old[:, :LANES].astype(BF16)), NEG_INF)
        m = jnp.maximum(jnp.max(s_w, axis=1, keepdims=True), jnp.max(s_f, axis=1, keepdims=True))
        p_w = jnp.exp(s_w - m)
        p_f = jnp.exp(s_f - m)
        l = jnp.sum(p_w, axis=1, keepdims=True) + jnp.sum(p_f, axis=1, keepdims=True)
        o = (_dot_nt(p_w.astype(BF16), v_new.astype(BF16)) +
             _dot_nt(p_f.astype(BF16), v_old[:, :LANES].astype(BF16)))
        o_ref[r] = o / l
        l_ref[r] = jnp.broadcast_to(m + jnp.log(l), (steps, HEAD_DIM))

    rows = k_ref.shape[0]
    group = min(rows, SAMPLE_ROW_GROUP)

    def body(i, carry):
        for u in range(group):
            one_row(i * group + u)
        return carry
    lax.fori_loop(0, rows // group, body, 0)


def _sample_attn(q, k_new_t, v_new_t, cache_kt, cache_vt, prev_out, a, g):
    win, dil = GROUPS[g]
    assert win // dil == BAND
    layers, rows, _, w = cache_kt.shape
    assert w == win
    steps = q.shape[1]
    rb = max(1, min(rows, CACHE_BLOCK_BYTES // (HEAD_DIM * win * 4)))
    cache = lambda: pl.BlockSpec((None, rb, HEAD_DIM, win), lambda i: (a, i, 0, 0))
    tok = lambda: pl.BlockSpec((rb, steps, HEAD_DIM), lambda i: (i, 0, 0))
    new = lambda: pl.BlockSpec((rb, HEAD_DIM, LANES), lambda i: (i, 0, 0))
    in_specs = [tok(), new(), new(), cache(), cache()]
    args = [q, k_new_t, v_new_t, cache_kt, cache_vt]
    aliases = {}
    if prev_out is not None:
        in_specs += [pl.BlockSpec(memory_space=pl.ANY)] * 2
        args += list(prev_out)
        aliases = {5: 0, 6: 1}
    return pl.pallas_call(
        functools.partial(_sample_attn_kernel, dil=dil, steps=steps),
        grid=(rows // rb,),
        in_specs=in_specs,
        out_specs=[cache(), cache(), tok(), tok()],
        out_shape=[jax.ShapeDtypeStruct(cache_kt.shape, F32)] * 2 +
                  [jax.ShapeDtypeStruct(q.shape, F32)] * 2,
        input_output_aliases=aliases,
        compiler_params=_params("parallel"),
        name=f"sample_attn_g{g}",
    )(*args)


def _merge_kernel(*refs, alpha):
    o_refs = refs[0:N_GROUPS]
    l_refs = refs[N_GROUPS:2 * N_GROUPS]
    x_ref, wo_ref, lng_ref, lnb_ref, out_ref, outb_ref = refs[2 * N_GROUPS:]
    lses = [r[...] for r in l_refs]
    top = functools.reduce(jnp.maximum, lses)
    es = [jnp.exp(l - top) for l in lses]
    den = functools.reduce(jnp.add, es)
    y = functools.reduce(jnp.add, [(e / den) * r[...] for e, r in zip(es, o_refs)])
    y = jnp.concatenate([y[t] for t in range(y.shape[0])], axis=1)
    m = _dot(y.astype(BF16), wo_ref[0])
    out = _layer_norm(alpha * x_ref[...] + m, lng_ref[0, 0], lnb_ref[0, 0])
    out_ref[...] = out
    outb_ref[...] = out.astype(BF16)


def _merge(outs, lses, x, w_o, ln_g, ln_b, a, li, alpha):
    t, d = x.shape
    tm = min(512, t)
    grp = lambda: pl.BlockSpec((GROUP_WIDTH // LANES, tm, LANES), lambda i: (0, i, 0))
    tok = lambda: pl.BlockSpec((tm, d), lambda i: (i, 0))
    return pl.pallas_call(
        functools.partial(_merge_kernel, alpha=alpha),
        grid=(t // tm,),
        in_specs=[grp() for _ in range(2 * N_GROUPS)] + [
            tok(), pl.BlockSpec((1, GROUP_WIDTH, d), lambda i: (a, 0, 0)),
            _ln_spec(d, li, 0, 1), _ln_spec(d, li, 0, 1)],
        out_specs=[tok(), tok()],
        out_shape=[jax.ShapeDtypeStruct((t, d), F32), jax.ShapeDtypeStruct((t, d), BF16)],
        compiler_params=_params("parallel"),
        name="merge_wo",
    )(*outs, *lses, x, w_o, ln_g, ln_b)


def _heads_first(x, nbatch, steps):
    x = x.reshape(steps, nbatch, HEADS, HEAD_DIM)
    return jnp.transpose(x, (1, 2, 0, 3)).reshape(nbatch * HEADS, steps, HEAD_DIM)


def _new_columns(x, nbatch, steps):
    x = x.reshape(steps, nbatch, HEADS, HEAD_DIM)
    x = jnp.transpose(x, (1, 2, 3, 0)).reshape(nbatch * HEADS, HEAD_DIM, steps)
    return jnp.pad(x, ((0, 0), (0, 0), (LANES - steps, 0)))


def _tokens_first(x, nbatch, steps):
    x = x.reshape(nbatch, HEADS // 2, 2, steps, HEAD_DIM)
    return jnp.transpose(x, (1, 3, 0, 2, 4)).reshape(HEADS // 2, steps * nbatch, LANES)


def kernel(x_prompt, x_sample, state_conv, cache_k_w128, cache_v_w128, cache_k_w512, cache_v_w512, cache_k_w2048, cache_v_w2048, ln_g, ln_b, w_pw1, b_pw1, w_dw, b_dw, cn_g, cn_b, w_pw2, b_pw2, w_qkv, w_o, w_ffn_gu, w_ffn_down, w_router, b_router, w_exp_gu, w_exp_down):
    batch, seq, d = x_prompt.shape
    nbatch, s_new, _ = x_sample.shape
    depth = ln_g.shape[0]
    alpha = (2 * depth) ** 0.25
    caches = ((cache_k_w128, cache_v_w128), (cache_k_w512, cache_v_w512),
              (cache_k_w2048, cache_v_w2048))
    caches_t = [[jnp.transpose(c, (0, 1, 3, 4, 2)).reshape(c.shape[0], nbatch * HEADS, HEAD_DIM, c.shape[2])
                 for c in kv] for kv in caches]

    w_pw1_h, w_pw2_h, w_qkv_h, w_o_h = (w.astype(BF16) for w in (w_pw1, w_pw2, w_qkv, w_o))
    w_ffn_gu_h, w_ffn_down_h = w_ffn_gu.astype(BF16), w_ffn_down.astype(BF16)
    w_exp_gu_h, w_exp_down_h = w_exp_gu.astype(BF16), w_exp_down.astype(BF16)
    row = lambda v: v[:, None, :]
    conv_wts = (w_dw, row(b_dw), row(cn_g), row(cn_b), w_pw2_h, row(b_pw2))
    ln_g4, ln_b4 = ln_g[:, :, None, :], ln_b[:, :, None, :]
    w_router_t = jnp.swapaxes(w_router, 1, 2)
    b_router3 = row(b_router)
    state_t = jnp.transpose(state_conv, (0, 2, 1, 3))

    pos_p = jnp.tile(jnp.arange(seq), batch)
    pos_s = jnp.repeat(PAST_LEN + jnp.arange(s_new), nbatch)
    dils = tuple(dil for _, dil in GROUPS)

    xp = x_prompt.reshape(batch * seq, d)
    xs = jnp.transpose(x_sample, (1, 0, 2)).reshape(s_new * nbatch, d)
    conv_p, conv_s = [], []
    kv_p = [[] for _ in range(2 * N_GROUPS)]
    kv_s = [None] * N_GROUPS
    for li in range(depth):
        a = li // 2
        if li % 2 == 0:
            wts = conv_wts + (ln_g4, ln_b4)
            up = _pw1_glu(xp, w_pw1_h, row(b_pw1), a)
            us = _pw1_glu(xs, w_pw1_h, row(b_pw1), a)
            xp = _conv_prompt(up, xp, batch, wts, a, li, alpha)
            xs, st = _conv_sample(us, xs, state_t, wts, a, li, alpha)
            conv_p.append(up.reshape(batch, seq, d)[:, seq - CONV_HIST:])
            conv_s.append(st)
            xp = _ffn(xp, w_ffn_gu_h, w_ffn_down_h, ln_g4, ln_b4, a, li, alpha)
            xs = _ffn(xs, w_ffn_gu_h, w_ffn_down_h, ln_g4, ln_b4, a, li, alpha)
        else:
            qkv_p = _qkv(xp, w_qkv_h, pos_p, a, batch, dils)
            qkv_s = _qkv(xs, w_qkv_h, pos_s, a, 1, (1,) * N_GROUPS)
            outs_p, lses_p, outs_s, lses_s = [], [], [], []
            for g, (win, dil) in enumerate(GROUPS):
                o, l = _band_attn(qkv_p[g], g)
                outs_p.append(o)
                lses_p.append(l)
                lc = seq // dil
                for j in range(2):
                    tail = qkv_p[g][:, :, lc - BAND:, (1 + j) * GROUP_WIDTH:(2 + j) * GROUP_WIDTH]
                    tail = jnp.swapaxes(tail, 1, 2).reshape(batch, win, HEADS, HEAD_DIM)
                    kv_p[2 * g + j].append(tail)
                part = lambda j: qkv_s[g][0, 0, :, j * GROUP_WIDTH:(j + 1) * GROUP_WIDTH]
                res = _sample_attn(_heads_first(part(0), nbatch, s_new),
                                   _new_columns(part(1), nbatch, s_new),
                                   _new_columns(part(2), nbatch, s_new),
                                   caches_t[g][0], caches_t[g][1], kv_s[g], a, g)
                kv_s[g] = res[:2]
                outs_s.append(_tokens_first(res[2], nbatch, s_new))
                lses_s.append(_tokens_first(res[3], nbatch, s_new))
            xp, xpb = _merge(outs_p, lses_p, xp, w_o_h, ln_g4, ln_b4, a, li, alpha)
            xs, xsb = _merge(outs_s, lses_s, xs, w_o_h, ln_g4, ln_b4, a, li, alpha)
            for stream in range(2):
                x, xb = (xp, xpb) if stream == 0 else (xs, xsb)
                combine, rank = _router(x, w_router_t, b_router3, a, min(MOE_TOKEN_TILE, x.shape[0]))
                y = _moe(xb, combine, rank, w_exp_gu_h, w_exp_down_h, a)
                x = _residual_ln(x, y, ln_g4, ln_b4, li, alpha)
                if stream == 0:
                    xp = x
                else:
                    xs = x

    kv_p_out = [jnp.stack(t) for t in kv_p]
    kv_s_out = []
    for g, (win, _) in enumerate(GROUPS):
        for j in range(2):
            ct = kv_s[g][j].reshape(-1, nbatch, HEADS, HEAD_DIM, win)
            kv_s_out.append(jnp.transpose(ct, (0, 1, 4, 2, 3)))
    conv_s_out = jnp.transpose(jnp.concatenate(conv_s, axis=0), (0, 2, 1, 3))
    y_sample = jnp.transpose(xs.reshape(s_new, nbatch, d), (1, 0, 2))
    return (xp.reshape(batch, seq, d), y_sample, jnp.stack(conv_p), conv_s_out,
            *kv_p_out, *kv_s_out)
```

```python
import functools

import jax
import jax.numpy as jnp
from jax import lax
from jax.experimental import pallas as pl
from jax.experimental.pallas import tpu as pltpu

GROUPS = ((128, 1), (512, 4), (2048, 16))
N_GROUPS = len(GROUPS)
HEADS = 8
HEAD_DIM = 64
GROUP_WIDTH = HEADS * HEAD_DIM
BAND = 128
CONV_WIDTH = 31
CONV_HIST = CONV_WIDTH - 1
CONV_HALO = 32
PAST_LEN = 2048
ROPE_THETA = 10000.0
LN_EPS = 1e-5
NEG_INF = -1e30
LANES = 128

VMEM_LIMIT_BYTES = 56 * 1024 * 1024
TOKEN_TILE = 1024
FF_TILE = 512
MOE_TOKEN_TILE = 2048
MOE_FF_TILE = 896
MOE_CHUNK = 256
MOE_SLOTS = 128
SLOT_SHIFT = MOE_SLOTS.bit_length() - 1
ROUTER_TILE = 512
CACHE_BLOCK_BYTES = 2 * 1024 * 1024
SAMPLE_ROW_GROUP = 4

F32 = jnp.float32
BF16 = jnp.bfloat16


def _params(*semantics):
    return pltpu.CompilerParams(dimension_semantics=semantics,
                                vmem_limit_bytes=VMEM_LIMIT_BYTES)


def _layer_norm(x, g, b):
    mu = jnp.mean(x, axis=-1, keepdims=True)
    xc = x - mu
    var = jnp.mean(xc * xc, axis=-1, keepdims=True)
    return xc * lax.rsqrt(var + LN_EPS) * g + b


def _sigmoid(x):
    return 1.0 / (1.0 + jnp.exp(-x))


def _dot(a, b):
    return jnp.dot(a, b, preferred_element_type=F32)


def _dot_nt(a, b):
    return lax.dot_general(a, b, (((1,), (1,)), ((), ())), preferred_element_type=F32)


def _token_tile(t):
    return min(TOKEN_TILE, t)


def _ln_spec(d, li, which, nargs):
    idx = (li, which, 0, 0)
    return pl.BlockSpec((1, 1, 1, d), {1: lambda i: idx, 2: lambda i, j: idx,
                                       3: lambda i, j, k: idx}[nargs])


def _pw1_glu_kernel(x_ref, w_ref, b_ref, u_ref):
    d = u_ref.shape[-1]
    x = x_ref[...].astype(BF16)
    a = _dot(x, w_ref[0, :, :d]) + b_ref[0, :, :d]
    g = _dot(x, w_ref[0, :, d:]) + b_ref[0, :, d:]
    u_ref[...] = a * _sigmoid(g)


def _pw1_glu(x, w_pw1, b_pw1, a):
    t, d = x.shape
    tm = _token_tile(t)
    return pl.pallas_call(
        _pw1_glu_kernel,
        grid=(t // tm,),
        in_specs=[pl.BlockSpec((tm, d), lambda i: (i, 0)),
                  pl.BlockSpec((1, d, 2 * d), lambda i: (a, 0, 0)),
                  pl.BlockSpec((1, 1, 2 * d), lambda i: (a, 0, 0))],
        out_specs=pl.BlockSpec((tm, d), lambda i: (i, 0)),
        out_shape=jax.ShapeDtypeStruct((t, d), F32),
        compiler_params=_params("parallel"),
        name="pw1_glu",
    )(x, w_pw1, b_pw1)


def _conv_tail(y, x, bdw_ref, cng_ref, cnb_ref, w2_ref, b2_ref, lng_ref, lnb_ref, alpha):
    y = y + bdw_ref[0]
    y = _layer_norm(y, cng_ref[0], cnb_ref[0])
    y = y * _sigmoid(y)
    m = _dot(y.astype(BF16), w2_ref[0]) + b2_ref[0]
    return _layer_norm(alpha * x + m, lng_ref[0, 0], lnb_ref[0, 0])


CONV_ROW_CHUNK = 32
CONV_LANE_CHUNK = 256
SUBLANES = 8


def _conv_prompt_kernel(um_ref, uh_ref, x_ref, wdw_ref, bdw_ref, cng_ref, cnb_ref, w2_ref,
                        b2_ref, lng_ref, lnb_ref, o_ref, win_ref, y_ref, *, alpha):
    i = pl.program_id(1)
    tt = um_ref.shape[1]
    win_ref[0:CONV_HALO, :] = jnp.where(i > 0, uh_ref[0], 0.0)
    win_ref[CONV_HALO:CONV_HALO + tt, :] = um_ref[0]
    lead = CONV_HALO - CONV_HIST
    span = 2 * CONV_ROW_CHUNK
    assert lead + CONV_WIDTH - 1 + CONV_ROW_CHUNK <= span
    for lc in range(0, um_ref.shape[2], CONV_LANE_CHUNK):
        lanes = slice(lc, lc + CONV_LANE_CHUNK)
        w = wdw_ref[0, :, lanes]
        for r in range(0, tt, CONV_ROW_CHUNK):
            slab = win_ref[r:r + span, lanes]
            acc = None
            for res in range(SUBLANES):
                rolled = slab if res == 0 else pltpu.roll(slab, span - res, 0)
                for j in range(CONV_WIDTH):
                    if (lead + j) % SUBLANES == res:
                        base = lead + j - res
                        term = w[j:j + 1] * rolled[base:base + CONV_ROW_CHUNK]
                        acc = term if acc is None else acc + term
            y_ref[r:r + CONV_ROW_CHUNK, lanes] = acc
    o_ref[...] = _conv_tail(y_ref[...], x_ref[...], bdw_ref, cng_ref, cnb_ref, w2_ref, b2_ref,
                            lng_ref, lnb_ref, alpha)


def _conv_prompt(u, x, batch, wts, a, li, alpha):
    w_dw, b_dw, cn_g, cn_b, w_pw2, b_pw2, ln_g, ln_b = wts
    t, d = x.shape
    seq = t // batch
    tt = min(512, seq)
    nt = seq // tt
    hb = tt // CONV_HALO
    u3 = u.reshape(batch, seq, d)
    vec = lambda: pl.BlockSpec((1, 1, d), lambda b, i: (a, 0, 0))
    return pl.pallas_call(
        functools.partial(_conv_prompt_kernel, alpha=alpha),
        grid=(batch, nt),
        in_specs=[pl.BlockSpec((1, tt, d), lambda b, i: (b, i, 0)),
                  pl.BlockSpec((1, CONV_HALO, d), lambda b, i: (b, jnp.maximum(i * hb - 1, 0), 0)),
                  pl.BlockSpec((tt, d), lambda b, i: (b * nt + i, 0)),
                  pl.BlockSpec((1, CONV_WIDTH, d), lambda b, i: (a, 0, 0)),
                  vec(), vec(), vec(),
                  pl.BlockSpec((1, d, d), lambda b, i: (a, 0, 0)),
                  vec(),
                  _ln_spec(d, li, 0, 2), _ln_spec(d, li, 0, 2)],
        out_specs=pl.BlockSpec((tt, d), lambda b, i: (b * nt + i, 0)),
        out_shape=jax.ShapeDtypeStruct((t, d), F32),
        scratch_shapes=[pltpu.VMEM((CONV_HALO + tt, d), F32), pltpu.VMEM((tt, d), F32)],
        compiler_params=_params("parallel", "arbitrary"),
        name="conv_prompt",
    )(u3, u3, x, w_dw, b_dw, cn_g, cn_b, w_pw2, b_pw2, ln_g, ln_b)


def _conv_sample_kernel(u_ref, st_ref, x_ref, wdw_ref, bdw_ref, cng_ref, cnb_ref, w2_ref,
                        b2_ref, lng_ref, lnb_ref, o_ref, ns_ref, win_ref, *, alpha):
    s, nb, d = u_ref.shape
    win_ref[0:CONV_HIST] = st_ref[0]
    win_ref[CONV_HIST:CONV_HIST + s] = u_ref[...]
    ns_ref[0] = win_ref[s:s + CONV_HIST]
    acc = wdw_ref[0, 0:1, :] * win_ref[0:s]
    for j in range(1, CONV_WIDTH):
        acc = acc + wdw_ref[0, j:j + 1, :] * win_ref[j:j + s]
    out = _conv_tail(acc.reshape(s * nb, d), x_ref[...].reshape(s * nb, d), bdw_ref, cng_ref,
                     cnb_ref, w2_ref, b2_ref, lng_ref, lnb_ref, alpha)
    o_ref[...] = out.reshape(s, nb, d)


def _conv_sample(u, x, state_t, wts, a, li, alpha):
    w_dw, b_dw, cn_g, cn_b, w_pw2, b_pw2, ln_g, ln_b = wts
    t, d = x.shape
    nbatch = state_t.shape[2]
    s = t // nbatch
    nb = min(16, nbatch)
    vec = lambda: pl.BlockSpec((1, 1, d), lambda i: (a, 0, 0))
    tok = lambda: pl.BlockSpec((s, nb, d), lambda i: (0, i, 0))
    x_new, st_new = pl.pallas_call(
        functools.partial(_conv_sample_kernel, alpha=alpha),
        grid=(nbatch // nb,),
        in_specs=[tok(),
                  pl.BlockSpec((1, CONV_HIST, nb, d), lambda i: (a, 0, i, 0)),
                  tok(),
                  pl.BlockSpec((1, CONV_WIDTH, d), lambda i: (a, 0, 0)),
                  vec(), vec(), vec(),
                  pl.BlockSpec((1, d, d), lambda i: (a, 0, 0)),
                  vec(),
                  _ln_spec(d, li, 0, 1), _ln_spec(d, li, 0, 1)],
        out_specs=[tok(), pl.BlockSpec((1, CONV_HIST, nb, d), lambda i: (0, 0, i, 0))],
        out_shape=[jax.ShapeDtypeStruct((s, nbatch, d), F32),
                   jax.ShapeDtypeStruct((1, CONV_HIST, nbatch, d), F32)],
        scratch_shapes=[pltpu.VMEM((CONV_HIST + s, nb, d), F32)],
        compiler_params=_params("parallel"),
        name="conv_sample",
    )(u.reshape(s, nbatch, d), state_t, x.reshape(s, nbatch, d), w_dw, b_dw, cn_g, cn_b,
      w_pw2, b_pw2, ln_g, ln_b)
    return x_new.reshape(t, d), st_new


def _ffn_kernel(x_ref, wg_ref, wu_ref, wd_ref, lng_ref, lnb_ref, o_ref, acc_ref, *, alpha):
    k = pl.program_id(1)

    @pl.when(k == 0)
    def _():
        acc_ref[...] = jnp.zeros_like(acc_ref)

    x = x_ref[...].astype(BF16)
    g = _dot(x, wg_ref[0])
    u = _dot(x, wu_ref[0])
    h = (g * _sigmoid(g) * u).astype(BF16)
    acc_ref[...] += _dot(h, wd_ref[0])

    @pl.when(k == pl.num_programs(1) - 1)
    def _():
        o_ref[...] = _layer_norm(alpha * x_ref[...] + acc_ref[...], lng_ref[0, 0], lnb_ref[0, 0])


def _ffn(x, w_gu, w_down, ln_g, ln_b, a, li, alpha):
    t, d = x.shape
    f = w_down.shape[1]
    tm = _token_tile(t)
    tf = min(FF_TILE, f)
    nf = f // tf
    return pl.pallas_call(
        functools.partial(_ffn_kernel, alpha=alpha),
        grid=(t // tm, nf),
        in_specs=[pl.BlockSpec((tm, d), lambda i, k: (i, 0)),
                  pl.BlockSpec((1, d, tf), lambda i, k: (a, 0, k)),
                  pl.BlockSpec((1, d, tf), lambda i, k: (a, 0, k + nf)),
                  pl.BlockSpec((1, tf, d), lambda i, k: (a, k, 0)),
                  _ln_spec(d, li, 1, 2), _ln_spec(d, li, 1, 2)],
        out_specs=pl.BlockSpec((tm, d), lambda i, k: (i, 0)),
        out_shape=jax.ShapeDtypeStruct((t, d), F32),
        scratch_shapes=[pltpu.VMEM((tm, d), F32)],
        compiler_params=_params("parallel", "arbitrary"),
        name="ffn_dense",
    )(x, w_gu, w_gu, w_down, ln_g, ln_b)


def _router_kernel(x_ref, wt_ref, b_ref, c_ref, r_ref, count_ref, *, tiles_per_group):
    i = pl.program_id(0)
    x = x_ref[...]
    tr = x.shape[0]
    n_exp = wt_ref.shape[1]
    lane = lax.broadcasted_iota(jnp.int32, (tr, LANES), 1).astype(F32)
    logits = jnp.where(lane < n_exp, b_ref[0], -jnp.inf)
    for e in range(n_exp):
        col = jnp.sum(x * wt_ref[0, e:e + 1, :], axis=1, keepdims=True)
        logits = logits + jnp.where(lane == e, col, 0.0)
    m1 = jnp.max(logits, axis=1, keepdims=True)
    i1 = jnp.min(jnp.where(logits == m1, lane, float(LANES)), axis=1, keepdims=True)
    rest = jnp.where(lane == i1, -jnp.inf, logits)
    m2 = jnp.max(rest, axis=1, keepdims=True)
    i2 = jnp.min(jnp.where(rest == m2, lane, float(LANES)), axis=1, keepdims=True)
    e2 = jnp.exp(m2 - m1)
    g1 = 1.0 / (1.0 + e2)
    g2 = e2 / (1.0 + e2)
    c_ref[...] = jnp.where(lane == i1, g1, 0.0) + jnp.where(lane == i2, g2, 0.0)

    @pl.when(i % tiles_per_group == 0)
    def _():
        count_ref[...] = jnp.zeros_like(count_ref)

    member = (lane == i1) | (lane == i2)
    below = (lax.broadcasted_iota(jnp.int32, (tr, tr), 0) >
             lax.broadcasted_iota(jnp.int32, (tr, tr), 1))
    earlier = _dot(below.astype(BF16), member.astype(BF16))
    r_ref[...] = jnp.where(member, earlier + count_ref[...], -1.0)
    count_ref[...] += jnp.sum(member.astype(F32), axis=0, keepdims=True)


def _router(x, w_router_t, b_router, a, moe_tile):
    t, d = x.shape
    n_exp = w_router_t.shape[1]
    tr = min(ROUTER_TILE, t)
    b_pad = jnp.pad(b_router, ((0, 0), (0, 0), (0, LANES - n_exp)))
    return pl.pallas_call(
        functools.partial(_router_kernel, tiles_per_group=moe_tile // tr),
        grid=(t // tr,),
        in_specs=[pl.BlockSpec((tr, d), lambda i: (i, 0)),
                  pl.BlockSpec((1, n_exp, d), lambda i: (a, 0, 0)),
                  pl.BlockSpec((1, 1, LANES), lambda i: (a, 0, 0))],
        out_specs=[pl.BlockSpec((tr, LANES), lambda i: (i, 0)),
                   pl.BlockSpec((tr, LANES), lambda i: (i, 0))],
        out_shape=[jax.ShapeDtypeStruct((t, LANES), F32)] * 2,
        scratch_shapes=[pltpu.VMEM((1, LANES), F32)],
        compiler_params=_params("arbitrary"),
        name="router",
    )(x, w_router_t, b_pad)


def _slot_rows(q):
    return pl.ds(pl.multiple_of(q * MOE_SLOTS, MOE_SLOTS), MOE_SLOTS)


def _moe_kernel(pre_ref, xb_ref, rt_ref, r_ref, c_ref, wg_ref, wu_ref, wd_ref, y_ref,
                xs_ref, ys_ref, *, n_chunks, chunk):
    i, e, k = pl.program_id(0), pl.program_id(1), pl.program_id(2)
    base = (i * pl.num_programs(1) + e) * (n_chunks + 1)
    n_sel = pre_ref[base + n_chunks]
    n_blocks = (n_sel + (MOE_SLOTS - 1)) >> SLOT_SHIFT

    def chunk_blocks(c):
        s0, s1 = pre_ref[base + c], pre_ref[base + c + 1]
        lo = s0 >> SLOT_SHIFT
        hi = (s1 + (MOE_SLOTS - 1)) >> SLOT_SHIFT
        return lo, jnp.where(s1 > s0, hi, lo)

    @pl.when((e == 0) & (k == 0))
    def _():
        y_ref[...] = jnp.zeros_like(y_ref)

    @pl.when(k == 0)
    def _():
        def clear(q, carry):
            xs_ref[_slot_rows(q), :] = jnp.zeros((MOE_SLOTS, xs_ref.shape[1]), BF16)
            return carry
        lax.fori_loop(0, n_blocks, clear, 0)
        for c in range(n_chunks):
            rank_row = rt_ref[0, :, c * chunk:(c + 1) * chunk]
            xc = xb_ref[c * chunk:(c + 1) * chunk, :]
            lo, hi = chunk_blocks(c)

            def gather(q, carry, rank_row=rank_row, xc=xc):
                slot = q * MOE_SLOTS + lax.broadcasted_iota(jnp.int32, (MOE_SLOTS, chunk), 0)
                onehot = (slot.astype(F32) == rank_row).astype(BF16)
                xs_ref[_slot_rows(q), :] += _dot(onehot, xc).astype(BF16)
                return carry
            lax.fori_loop(lo, hi, gather, 0)

    def ffn(start, size):
        rows = pl.ds(pl.multiple_of(start, MOE_SLOTS), size)
        xq = xs_ref[rows, :]
        g = _dot(xq, wg_ref[0, 0])
        u = _dot(xq, wu_ref[0, 0])
        h = (g * _sigmoid(g) * u).astype(BF16)
        part = _dot(h, wd_ref[0, 0])

        @pl.when(k == 0)
        def _():
            ys_ref[rows, :] = part

        @pl.when(k > 0)
        def _():
            ys_ref[rows, :] += part

    def ffn_quad(p, carry):
        ffn(p * (4 * MOE_SLOTS), 4 * MOE_SLOTS)
        return carry
    n_quads = n_blocks >> 2
    lax.fori_loop(0, n_quads, ffn_quad, 0)

    @pl.when((n_blocks & 2) != 0)
    def _():
        ffn(n_quads * (4 * MOE_SLOTS), 2 * MOE_SLOTS)

    @pl.when((n_blocks & 1) != 0)
    def _():
        ffn((n_blocks - 1) * MOE_SLOTS, MOE_SLOTS)

    @pl.when(k == pl.num_programs(2) - 1)
    def _():
        for c in range(n_chunks):
            tok = slice(c * chunk, (c + 1) * chunk)
            lane = lax.broadcasted_iota(jnp.int32, (chunk, LANES), 1)
            rank_col = jnp.sum(jnp.where(lane == e, r_ref[tok, :], 0.0), axis=1, keepdims=True)
            gate_col = jnp.sum(jnp.where(lane == e, c_ref[tok, :], 0.0), axis=1, keepdims=True)
            lo, hi = chunk_blocks(c)

            def scatter(q, carry, tok=tok, rank_col=rank_col, gate_col=gate_col):
                slot = q * MOE_SLOTS + lax.broadcasted_iota(jnp.int32, (chunk, MOE_SLOTS), 1)
                onehot = (rank_col == slot.astype(F32)).astype(BF16)
                y_ref[tok, :] += gate_col * _dot(onehot, ys_ref[_slot_rows(q), :].astype(BF16))
                return carry
            lax.fori_loop(lo, hi, scatter, 0)


def _moe(xb, combine, rank, w_gu, w_down, a):
    t, d = xb.shape
    n_exp, f = w_down.shape[1], w_down.shape[2]
    tm = min(MOE_TOKEN_TILE, t)
    nt = t // tm
    chunk = min(MOE_CHUNK, tm)
    n_chunks = tm // chunk
    cap = max(-(-tm // MOE_SLOTS), 4) * MOE_SLOTS
    tf = MOE_FF_TILE if f % MOE_FF_TILE == 0 else min(FF_TILE, f)
    nf = f // tf
    member = (rank[:, :n_exp] >= 0).astype(jnp.int32)
    per_chunk = member.reshape(nt, n_chunks, chunk, n_exp).sum(axis=2)
    pre = jnp.concatenate([jnp.zeros((nt, 1, n_exp), jnp.int32), jnp.cumsum(per_chunk, axis=1)], axis=1)
    pre = jnp.swapaxes(pre, 1, 2).reshape(-1)
    rank_t = jnp.swapaxes(rank[:, :n_exp], 0, 1)[:, None, :]
    grid_spec = pltpu.PrefetchScalarGridSpec(
        num_scalar_prefetch=1,
        grid=(nt, n_exp, nf),
        in_specs=[pl.BlockSpec((tm, d), lambda i, e, k, pre: (i, 0)),
                  pl.BlockSpec((1, 1, tm), lambda i, e, k, pre: (e, 0, i)),
                  pl.BlockSpec((tm, LANES), lambda i, e, k, pre: (i, 0)),
                  pl.BlockSpec((tm, LANES), lambda i, e, k, pre: (i, 0)),
                  pl.BlockSpec((1, 1, d, tf), lambda i, e, k, pre: (a, e, 0, k)),
                  pl.BlockSpec((1, 1, d, tf), lambda i, e, k, pre: (a, e, 0, k + nf)),
                  pl.BlockSpec((1, 1, tf, d), lambda i, e, k, pre: (a, e, k, 0))],
        out_specs=pl.BlockSpec((tm, d), lambda i, e, k, pre: (i, 0)),
        scratch_shapes=[pltpu.VMEM((cap, d), BF16), pltpu.VMEM((cap, d), F32)])
    return pl.pallas_call(
        functools.partial(_moe_kernel, n_chunks=n_chunks, chunk=chunk),
        grid_spec=grid_spec,
        out_shape=jax.ShapeDtypeStruct((t, d), F32),
        compiler_params=_params("parallel", "arbitrary", "arbitrary"),
        name="moe_sparse",
    )(pre, xb, rank_t, rank, combine, w_gu, w_gu, w_down)


def _residual_ln_kernel(x_ref, y_ref, lng_ref, lnb_ref, o_ref, *, alpha):
    o_ref[...] = _layer_norm(alpha * x_ref[...] + y_ref[...], lng_ref[0, 0], lnb_ref[0, 0])


def _residual_ln(x, y, ln_g, ln_b, li, alpha):
    t, d = x.shape
    tm = _token_tile(t)
    tok = lambda: pl.BlockSpec((tm, d), lambda i: (i, 0))
    return pl.pallas_call(
        functools.partial(_residual_ln_kernel, alpha=alpha),
        grid=(t // tm,),
        in_specs=[tok(), tok(), _ln_spec(d, li, 1, 1), _ln_spec(d, li, 1, 1)],
        out_specs=tok(),
        out_shape=jax.ShapeDtypeStruct((t, d), F32),
        compiler_params=_params("parallel"),
        name="residual_ln",
    )(x, y, ln_g, ln_b)


def _qkv_kernel(x_ref, w_ref, cos_ref, sin_ref, *refs, dils):
    o_refs, z_ref = refs[:N_GROUPS], refs[N_GROUPS]
    j = pl.program_id(1)
    g = j // 3
    kind = j - 3 * g
    z = _dot(x_ref[...].astype(BF16), w_ref[0])
    tm, width = z.shape

    @pl.when(kind < 2)
    def _():
        cos, sin = cos_ref[...], sin_ref[...]
        lane = lax.broadcasted_iota(jnp.int32, cos.shape, 1)
        half = HEAD_DIM // 2
        first = (lane & (HEAD_DIM - 1)) < half
        scale = jnp.where(kind == 0, HEAD_DIM ** -0.5, 1.0)
        for t in range(width // LANES):
            zt = z[:, t * LANES:(t + 1) * LANES]
            partner = jnp.where(first, pltpu.roll(zt, LANES - half, 1), pltpu.roll(zt, half, 1))
            z_ref[t] = (zt * cos + partner * sin) * scale

    @pl.when(kind == 2)
    def _():
        for t in range(width // LANES):
            z_ref[t] = z[:, t * LANES:(t + 1) * LANES]

    for gi, dil in enumerate(dils):
        @pl.when(g == gi)
        def _(gi=gi, dil=dil):
            for t in range(width // LANES):
                lanes = slice(t * LANES, (t + 1) * LANES)
                if dil == 1:
                    o_refs[gi][0, :, lanes] = z_ref[t]
                else:
                    for c in range(dil):
                        o_refs[gi][c, :, lanes] = z_ref[t, pl.ds(c, tm // dil, stride=dil), :]


def _rope_tables(pos):
    half = HEAD_DIM // 2
    inv = ROPE_THETA ** (-jnp.arange(half, dtype=F32) / half)
    ang = pos.astype(F32)[:, None] * inv[None, :]
    cos, sin = jnp.cos(ang), jnp.sin(ang)
    cos2 = jnp.concatenate([cos, cos], axis=1)
    sin2 = jnp.concatenate([-sin, sin], axis=1)
    return jnp.concatenate([cos2, cos2], axis=1), jnp.concatenate([sin2, sin2], axis=1)


def _qkv(x, w_qkv, pos, a, nseq, dils):
    t, d = x.shape
    seq = t // nseq
    tm = _token_tile(seq)
    tiles = seq // tm
    cos, sin = _rope_tables(pos)

    def out_spec(gi, dil):
        return pl.BlockSpec(
            (None, dil, tm // dil, GROUP_WIDTH),
            lambda i, j: (i // tiles, 0, i % tiles, jnp.clip(j - 3 * gi, 0, 2)))

    def w_col(i, j):
        return (a, 0, (j % 3) * N_GROUPS + j // 3)

    return pl.pallas_call(
        functools.partial(_qkv_kernel, dils=dils),
        grid=(t // tm, 3 * N_GROUPS),
        in_specs=[pl.BlockSpec((tm, d), lambda i, j: (i, 0)),
                  pl.BlockSpec((1, d, GROUP_WIDTH), w_col),
                  pl.BlockSpec((tm, 2 * HEAD_DIM), lambda i, j: (i, 0)),
                  pl.BlockSpec((tm, 2 * HEAD_DIM), lambda i, j: (i, 0))],
        out_specs=[out_spec(gi, dil) for gi, dil in enumerate(dils)],
        out_shape=[jax.ShapeDtypeStruct((nseq, dil, seq // dil, 3 * GROUP_WIDTH), F32)
                   for dil in dils],
        scratch_shapes=[pltpu.VMEM((GROUP_WIDTH // LANES, tm, LANES), F32)],
        compiler_params=_params("parallel", "arbitrary"),
        name="qkv_rope",
    )(x, w_qkv, cos, sin)


def _band_attn_kernel(q_ref, kp_ref, kc_ref, vp_ref, vc_ref, o_ref, l_ref, *, dil):
    n, c = pl.program_id(1), pl.program_id(2)
    tq = q_ref.shape[0]
    qi = lax.broadcasted_iota(jnp.int32, (tq, 2 * tq), 0) + tq
    ki = lax.broadcasted_iota(jnp.int32, (tq, 2 * tq), 1)
    dist = qi - ki
    mask = (dist >= 0) & (dist <= tq) & ((n > 0) | (ki >= tq))
    q = q_ref[...]
    k = jnp.concatenate([kp_ref[...], kc_ref[...]], axis=0).astype(BF16)
    v = jnp.concatenate([vp_ref[...], vc_ref[...]], axis=0).astype(BF16)
    lower = lax.broadcasted_iota(jnp.int32, (tq, LANES), 1) < HEAD_DIM
    outs, lses = [], []
    for t in range(GROUP_WIDTH // LANES):
        sl = slice(t * LANES, (t + 1) * LANES)
        o_t = lse_t = None
        for own in (lower, ~lower):
            qm = jnp.where(own, q[:, sl], 0.0).astype(BF16)
            s = jnp.where(mask, _dot_nt(qm, k[:, sl]), NEG_INF)
            m = jnp.max(s, axis=1, keepdims=True)
            p = jnp.exp(s - m)
            l = jnp.sum(p, axis=1, keepdims=True)
            o_h = _dot(p.astype(BF16), v[:, sl]) / l
            lse_h = jnp.broadcast_to(m + jnp.log(l), (tq, LANES))
            o_t = o_h if o_t is None else jnp.where(lower, o_t, o_h)
            lse_t = lse_h if lse_t is None else jnp.where(lower, lse_t, lse_h)
        outs.append(o_t)
        lses.append(lse_t)
    if dil == 1:
        for t in range(len(outs)):
            o_ref[t] = outs[t]
            l_ref[t] = lses[t]
    else:
        for cc in range(dil):
            @pl.when(c == cc)
            def _(cc=cc):
                for t in range(len(outs)):
                    o_ref[t, pl.ds(cc, tq, stride=dil), :] = outs[t]
                    l_ref[t, pl.ds(cc, tq, stride=dil), :] = lses[t]


def _band_attn(qkv_g, g):
    win, dil = GROUPS[g]
    assert win // dil == BAND
    batch, _, lc, _ = qkv_g.shape
    assert lc % BAND == 0
    nblk = lc // BAND
    tiles = GROUP_WIDTH // LANES
    prev = lambda n: jnp.maximum(n - 1, 0)
    blk = (None, None, BAND, GROUP_WIDTH)
    out = pl.BlockSpec((tiles, BAND * dil, LANES), lambda b, n, c: (0, b * nblk + n, 0))
    return pl.pallas_call(
        functools.partial(_band_attn_kernel, dil=dil),
        grid=(batch, nblk, dil),
        in_specs=[pl.BlockSpec(blk, lambda b, n, c: (b, c, n, 0)),
                  pl.BlockSpec(blk, lambda b, n, c: (b, c, prev(n), 1)),
                  pl.BlockSpec(blk, lambda b, n, c: (b, c, n, 1)),
                  pl.BlockSpec(blk, lambda b, n, c: (b, c, prev(n), 2)),
                  pl.BlockSpec(blk, lambda b, n, c: (b, c, n, 2))],
        out_specs=[out, out],
        out_shape=[jax.ShapeDtypeStruct((tiles, batch * lc * dil, LANES), F32)] * 2,
        compiler_params=_params("parallel", "arbitrary", "arbitrary"),
        name=f"band_attn_g{g}",
    )(qkv_g, qkv_g, qkv_g, qkv_g, qkv_g)


def _sample_attn_kernel(q_ref, kn_ref, vn_ref, k_ref, v_ref, *refs, dil, steps):
    ko_ref, vo_ref, o_ref, l_ref = refs[-4:]
    win = k_ref.shape[-1]
    last = win - LANES
    lane_w = lax.broadcasted_iota(jnp.int32, (steps, win), 1)
    lane_f = lax.broadcasted_iota(jnp.int32, (steps, LANES), 1)
    step_w = lax.broadcasted_iota(jnp.int32, (steps, win), 0)
    step_f = lax.broadcasted_iota(jnp.int32, (steps, LANES), 0)
    row_w = lane_w + steps
    mask_w = (((row_w - step_w) & (dil - 1)) == 0) & (row_w <= win + step_w)
    mask_f = (lane_f < steps) & (lane_f >= step_f) & (((lane_f - step_f) & (dil - 1)) == 0)
    new_lane = lax.broadcasted_iota(jnp.int32, (HEAD_DIM, LANES), 1) >= LANES - steps

    def shifted(old, new):
        moved = pltpu.roll(old, win - steps, 1)
        tail = jnp.where(new_lane, new, moved[:, last:])
        return tail if last == 0 else jnp.concatenate([moved[:, :last], tail], axis=1)

    def one_row(r):
        k_old, v_old = k_ref[r], v_ref[r]
        k_new = shifted(k_old, kn_ref[r])
        v_new = shifted(v_old, vn_ref[r])
        ko_ref[r] = k_new
        vo_ref[r] = v_new
        q = q_ref[r].astype(BF16)
        s_w = jnp.where(mask_w, _dot(q, k_new.astype(BF16)), NEG_INF)
        s_f = jnp.where(mask_f, _dot(q, k_old[:, :LANES].astype(BF16)), NEG_INF)
        m = jnp.maximum(jnp.max(s_w, axis=1, keepdims=True), jnp.max(s_f, axis=1, keepdims=True))
        p_w = jnp.exp(s_w - m)
        p_f = jnp.exp(s_f - m)
        l = jnp.sum(p_w, axis=1, keepdims=True) + jnp.sum(p_f, axis=1, keepdims=True)
        o = (_dot_nt(p_w.astype(BF16), v_new.astype(BF16)) +
             _dot_nt(p_f.astype(BF16), v_old[:, :LANES].astype(BF16)))
        o_ref[r] = o / l
        l_ref[r] = jnp.broadcast_to(m + jnp.log(l), (steps, HEAD_DIM))

    rows = k_ref.shape[0]
    group = min(rows, SAMPLE_ROW_GROUP)

    def body(i, carry):
        for u in range(group):
            one_row(i * group + u)
        return carry
    lax.fori_loop(0, rows // group, body, 0)


def _sample_attn(q, k_new_t, v_new_t, cache_kt, cache_vt, prev_out, a, g):
    win, dil = GROUPS[g]
    assert win // dil == BAND
    layers, rows, _, w = cache_kt.shape
    assert w == win
    steps = q.shape[1]
    rb = max(1, min(rows, CACHE_BLOCK_BYTES // (HEAD_DIM * win * 4)))
    cache = lambda: pl.BlockSpec((None, rb, HEAD_DIM, win), lambda i: (a, i, 0, 0))
    tok = lambda: pl.BlockSpec((rb, steps, HEAD_DIM), lambda i: (i, 0, 0))
    new = lambda: pl.BlockSpec((rb, HEAD_DIM, LANES), lambda i: (i, 0, 0))
    in_specs = [tok(), new(), new(), cache(), cache()]
    args = [q, k_new_t, v_new_t, cache_kt, cache_vt]
    aliases = {}
    if prev_out is not None:
        in_specs += [pl.BlockSpec(memory_space=pl.ANY)] * 2
        args += list(prev_out)
        aliases = {5: 0, 6: 1}
    return pl.pallas_call(
        functools.partial(_sample_attn_kernel, dil=dil, steps=steps),
        grid=(rows // rb,),
        in_specs=in_specs,
        out_specs=[cache(), cache(), tok(), tok()],
        out_shape=[jax.ShapeDtypeStruct(cache_kt.shape, F32)] * 2 +
                  [jax.ShapeDtypeStruct(q.shape, F32)] * 2,
        input_output_aliases=aliases,
        compiler_params=_params("parallel"),
        name=f"sample_attn_g{g}",
    )(*args)


def _merge_kernel(*refs, alpha):
    o_refs = refs[0:N_GROUPS]
    l_refs = refs[N_GROUPS:2 * N_GROUPS]
    x_ref, wo_ref, lng_ref, lnb_ref, out_ref, outb_ref = refs[2 * N_GROUPS:]
    lses = [r[...] for r in l_refs]
    top = functools.reduce(jnp.maximum, lses)
    es = [jnp.exp(l - top) for l in lses]
    den = functools.reduce(jnp.add, es)
    y = functools.reduce(jnp.add, [(e / den) * r[...] for e, r in zip(es, o_refs)])
    y = jnp.concatenate([y[t] for t in range(y.shape[0])], axis=1)
    m = _dot(y.astype(BF16), wo_ref[0])
    out = _layer_norm(alpha * x_ref[...] + m, lng_ref[0, 0], lnb_ref[0, 0])
    out_ref[...] = out
    outb_ref[...] = out.astype(BF16)


def _merge(outs, lses, x, w_o, ln_g, ln_b, a, li, alpha):
    t, d = x.shape
    tm = min(512, t)
    grp = lambda: pl.BlockSpec((GROUP_WIDTH // LANES, tm, LANES), lambda i: (0, i, 0))
    tok = lambda: pl.BlockSpec((tm, d), lambda i: (i, 0))
    return pl.pallas_call(
        functools.partial(_merge_kernel, alpha=alpha),
        grid=(t // tm,),
        in_specs=[grp() for _ in range(2 * N_GROUPS)] + [
            tok(), pl.BlockSpec((1, GROUP_WIDTH, d), lambda i: (a, 0, 0)),
            _ln_spec(d, li, 0, 1), _ln_spec(d, li, 0, 1)],
        out_specs=[tok(), tok()],
        out_shape=[jax.ShapeDtypeStruct((t, d), F32), jax.ShapeDtypeStruct((t, d), BF16)],
        compiler_params=_params("parallel"),
        name="merge_wo",
    )(*outs, *lses, x, w_o, ln_g, ln_b)


def _heads_first(x, nbatch, steps):
    x = x.reshape(steps, nbatch, HEADS, HEAD_DIM)
    return jnp.transpose(x, (1, 2, 0, 3)).reshape(nbatch * HEADS, steps, HEAD_DIM)


def _new_columns(x, nbatch, steps):
    x = x.reshape(steps, nbatch, HEADS, HEAD_DIM)
    x = jnp.transpose(x, (1, 2, 3, 0)).reshape(nbatch * HEADS, HEAD_DIM, steps)
    return jnp.pad(x, ((0, 0), (0, 0), (LANES - steps, 0)))


def _tokens_first(x, nbatch, steps):
    x = x.reshape(nbatch, HEADS // 2, 2, steps, HEAD_DIM)
    return jnp.transpose(x, (1, 3, 0, 2, 4)).reshape(HEADS // 2, steps * nbatch, LANES)


def kernel(x_prompt, x_sample, state_conv, cache_k_w128, cache_v_w128, cache_k_w512, cache_v_w512, cache_k_w2048, cache_v_w2048, ln_g, ln_b, w_pw1, b_pw1, w_dw, b_dw, cn_g, cn_b, w_pw2, b_pw2, w_qkv, w_o, w_ffn_gu, w_ffn_down, w_router, b_router, w_exp_gu, w_exp_down):
    batch, seq, d = x_prompt.shape
    nbatch, s_new, _ = x_sample.shape
    depth = ln_g.shape[0]
    alpha = (2 * depth) ** 0.25
    caches = ((cache_k_w128, cache_v_w128), (cache_k_w512, cache_v_w512),
              (cache_k_w2048, cache_v_w2048))
    caches_t = [[jnp.transpose(c, (0, 1, 3, 4, 2)).reshape(c.shape[0], nbatch * HEADS, HEAD_DIM, c.shape[2])
                 for c in kv] for kv in caches]

    w_pw1_h, w_pw2_h, w_qkv_h, w_o_h = (w.astype(BF16) for w in (w_pw1, w_pw2, w_qkv, w_o))
    w_ffn_gu_h, w_ffn_down_h = w_ffn_gu.astype(BF16), w_ffn_down.astype(BF16)
    w_exp_gu_h, w_exp_down_h = w_exp_gu.astype(BF16), w_exp_down.astype(BF16)
    row = lambda v: v[:, None, :]
    conv_wts = (w_dw, row(b_dw), row(cn_g), row(cn_b), w_pw2_h, row(b_pw2))
    ln_g4, ln_b4 = ln_g[:, :, None, :], ln_b[:, :, None, :]
    w_router_t = jnp.swapaxes(w_router, 1, 2)
    b_router3 = row(b_router)
    state_t = jnp.transpose(state_conv, (0, 2, 1, 3))

    pos_p = jnp.tile(jnp.arange(seq), batch)
    pos_s = jnp.repeat(PAST_LEN + jnp.arange(s_new), nbatch)
    dils = tuple(dil for _, dil in GROUPS)

    xp = x_prompt.reshape(batch * seq, d)
    xs = jnp.transpose(x_sample, (1, 0, 2)).reshape(s_new * nbatch, d)
    conv_p, conv_s = [], []
    kv_p = [[] for _ in range(2 * N_GROUPS)]
    kv_s = [None] * N_GROUPS
    for li in range(depth):
        a = li // 2
        if li % 2 == 0:
            wts = conv_wts + (ln_g4, ln_b4)
            up = _pw1_glu(xp, w_pw1_h, row(b_pw1), a)
            us = _pw1_glu(xs, w_pw1_h, row(b_pw1), a)
            xp = _conv_prompt(up, xp, batch, wts, a, li, alpha)
            xs, st = _conv_sample(us, xs, state_t, wts, a, li, alpha)
            conv_p.append(up.reshape(batch, seq, d)[:, seq - CONV_HIST:])
            conv_s.append(st)
            xp = _ffn(xp, w_ffn_gu_h, w_ffn_down_h, ln_g4, ln_b4, a, li, alpha)
            xs = _ffn(xs, w_ffn_gu_h, w_ffn_down_h, ln_g4, ln_b4, a, li, alpha)
        else:
            qkv_p = _qkv(xp, w_qkv_h, pos_p, a, batch, dils)
            qkv_s = _qkv(xs, w_qkv_h, pos_s, a, 1, (1,) * N_GROUPS)
            outs_p, lses_p, outs_s, lses_s = [], [], [], []
            for g, (win, dil) in enumerate(GROUPS):
                o, l = _band_attn(qkv_p[g], g)
                outs_p.append(o)
                lses_p.append(l)
                lc = seq // dil
                for j in range(2):
                    tail = qkv_p[g][:, :, lc - BAND:, (1 + j) * GROUP_WIDTH:(2 + j) * GROUP_WIDTH]
                    tail = jnp.swapaxes(tail, 1, 2).reshape(batch, win, HEADS, HEAD_DIM)
                    kv_p[2 * g + j].append(tail)
                part = lambda j: qkv_s[g][0, 0, :, j * GROUP_WIDTH:(j + 1) * GROUP_WIDTH]
                res = _sample_attn(_heads_first(part(0), nbatch, s_new),
                                   _new_columns(part(1), nbatch, s_new),
                                   _new_columns(part(2), nbatch, s_new),
                                   caches_t[g][0], caches_t[g][1], kv_s[g], a, g)
                kv_s[g] = res[:2]
                outs_s.append(_tokens_first(res[2], nbatch, s_new))
                lses_s.append(_tokens_first(res[3], nbatch, s_new))
            xp, xpb = _merge(outs_p, lses_p, xp, w_o_h, ln_g4, ln_b4, a, li, alpha)
            xs, xsb = _merge(outs_s, lses_s, xs, w_o_h, ln_g4, ln_b4, a, li, alpha)
            for stream in range(2):
                x, xb = (xp, xpb) if stream == 0 else (xs, xsb)
                combine, rank = _router(x, w_router_t, b_router3, a, min(MOE_TOKEN_TILE, x.shape[0]))
                y = _moe(xb, combine, rank, w_exp_gu_h, w_exp_down_h, a)
                x = _residual_ln(x, y, ln_g4, ln_b4, li, alpha)
                if stream == 0:
                    xp = x
                else:
                    xs = x

    kv_p_out = [jnp.stack(t) for t in kv_p]
    kv_s_out = []
    for g, (win, _) in enumerate(GROUPS):
        for j in range(2):
            ct = kv_s[g][j].reshape(-1, nbatch, HEADS, HEAD_DIM, win)
            kv_s_out.append(jnp.transpose(ct, (0, 1, 4, 2, 3)))
    conv_s_out = jnp.transpose(jnp.concatenate(conv_s, axis=0), (0, 2, 1, 3))
    y_sample = jnp.transpose(xs.reshape(s_new, nbatch, d), (1, 0, 2))
    return (xp.reshape(batch, seq, d), y_sample, jnp.stack(conv_p), conv_s_out,
            *kv_p_out, *kv_s_out)
```

```python
import functools

import jax
import jax.numpy as jnp
from jax import lax
from jax.experimental import pallas as pl
from jax.experimental.pallas import tpu as pltpu

GROUPS = ((128, 1), (512, 4), (2048, 16))
N_GROUPS = len(GROUPS)
HEADS = 8
HEAD_DIM = 64
GROUP_WIDTH = HEADS * HEAD_DIM
BAND = 128
CONV_WIDTH = 31
CONV_HIST = CONV_WIDTH - 1
CONV_HALO = 32
PAST_LEN = 2048
ROPE_THETA = 10000.0
LN_EPS = 1e-5
NEG_INF = -1e30
LANES = 128

VMEM_LIMIT_BYTES = 56 * 1024 * 1024
TOKEN_TILE = 1024
FF_TILE = 512
MOE_TOKEN_TILE = 2048
MOE_FF_TILE = 896
MOE_CHUNK = 256
MOE_SLOTS = 128
SLOT_SHIFT = MOE_SLOTS.bit_length() - 1
ROUTER_TILE = 512
CACHE_BLOCK_BYTES = 2 * 1024 * 1024

F32 = jnp.float32
BF16 = jnp.bfloat16


def _params(*semantics):
    return pltpu.CompilerParams(dimension_semantics=semantics,
                                vmem_limit_bytes=VMEM_LIMIT_BYTES)


def _layer_norm(x, g, b):
    mu = jnp.mean(x, axis=-1, keepdims=True)
    xc = x - mu
    var = jnp.mean(xc * xc, axis=-1, keepdims=True)
    return xc * lax.rsqrt(var + LN_EPS) * g + b


def _sigmoid(x):
    return 1.0 / (1.0 + jnp.exp(-x))


def _dot(a, b):
    return jnp.dot(a, b, preferred_element_type=F32)


def _dot_nt(a, b):
    return lax.dot_general(a, b, (((1,), (1,)), ((), ())), preferred_element_type=F32)


def _token_tile(t):
    return min(TOKEN_TILE, t)


def _ln_spec(d, li, which, nargs):
    idx = (li, which, 0, 0)
    return pl.BlockSpec((1, 1, 1, d), {1: lambda i: idx, 2: lambda i, j: idx,
                                       3: lambda i, j, k: idx}[nargs])


def _pw1_glu_kernel(x_ref, w_ref, b_ref, u_ref):
    d = u_ref.shape[-1]
    x = x_ref[...].astype(BF16)
    a = _dot(x, w_ref[0, :, :d]) + b_ref[0, :, :d]
    g = _dot(x, w_ref[0, :, d:]) + b_ref[0, :, d:]
    u_ref[...] = a * _sigmoid(g)


def _pw1_glu(x, w_pw1, b_pw1, a):
    t, d = x.shape
    tm = _token_tile(t)
    return pl.pallas_call(
        _pw1_glu_kernel,
        grid=(t // tm,),
        in_specs=[pl.BlockSpec((tm, d), lambda i: (i, 0)),
                  pl.BlockSpec((1, d, 2 * d), lambda i: (a, 0, 0)),
                  pl.BlockSpec((1, 1, 2 * d), lambda i: (a, 0, 0))],
        out_specs=pl.BlockSpec((tm, d), lambda i: (i, 0)),
        out_shape=jax.ShapeDtypeStruct((t, d), F32),
        compiler_params=_params("parallel"),
        name="pw1_glu",
    )(x, w_pw1, b_pw1)


def _conv_tail(y, x, bdw_ref, cng_ref, cnb_ref, w2_ref, b2_ref, lng_ref, lnb_ref, alpha):
    y = y + bdw_ref[0]
    y = _layer_norm(y, cng_ref[0], cnb_ref[0])
    y = y * _sigmoid(y)
    m = _dot(y.astype(BF16), w2_ref[0]) + b2_ref[0]
    return _layer_norm(alpha * x + m, lng_ref[0, 0], lnb_ref[0, 0])


CONV_ROW_CHUNK = 32
CONV_LANE_CHUNK = 256
SUBLANES = 8


def _conv_prompt_kernel(um_ref, uh_ref, x_ref, wdw_ref, bdw_ref, cng_ref, cnb_ref, w2_ref,
                        b2_ref, lng_ref, lnb_ref, o_ref, win_ref, y_ref, *, alpha):
    i = pl.program_id(1)
    tt = um_ref.shape[1]
    win_ref[0:CONV_HALO, :] = jnp.where(i > 0, uh_ref[0], 0.0)
    win_ref[CONV_HALO:CONV_HALO + tt, :] = um_ref[0]
    lead = CONV_HALO - CONV_HIST
    span = 2 * CONV_ROW_CHUNK
    assert lead + CONV_WIDTH - 1 + CONV_ROW_CHUNK <= span
    for lc in range(0, um_ref.shape[2], CONV_LANE_CHUNK):
        lanes = slice(lc, lc + CONV_LANE_CHUNK)
        w = wdw_ref[0, :, lanes]
        for r in range(0, tt, CONV_ROW_CHUNK):
            slab = win_ref[r:r + span, lanes]
            acc = None
            for res in range(SUBLANES):
                rolled = slab if res == 0 else pltpu.roll(slab, span - res, 0)
                for j in range(CONV_WIDTH):
                    if (lead + j) % SUBLANES == res:
                        base = lead + j - res
                        term = w[j:j + 1] * rolled[base:base + CONV_ROW_CHUNK]
                        acc = term if acc is None else acc + term
            y_ref[r:r + CONV_ROW_CHUNK, lanes] = acc
    o_ref[...] = _conv_tail(y_ref[...], x_ref[...], bdw_ref, cng_ref, cnb_ref, w2_ref, b2_ref,
                            lng_ref, lnb_ref, alpha)


def _conv_prompt(u, x, batch, wts, a, li, alpha):
    w_dw, b_dw, cn_g, cn_b, w_pw2, b_pw2, ln_g, ln_b = wts
    t, d = x.shape
    seq = t // batch
    tt = min(512, seq)
    nt = seq // tt
    hb = tt // CONV_HALO
    u3 = u.reshape(batch, seq, d)
    vec = lambda: pl.BlockSpec((1, 1, d), lambda b, i: (a, 0, 0))
    return pl.pallas_call(
        functools.partial(_conv_prompt_kernel, alpha=alpha),
        grid=(batch, nt),
        in_specs=[pl.BlockSpec((1, tt, d), lambda b, i: (b, i, 0)),
                  pl.BlockSpec((1, CONV_HALO, d), lambda b, i: (b, jnp.maximum(i * hb - 1, 0), 0)),
                  pl.BlockSpec((tt, d), lambda b, i: (b * nt + i, 0)),
                  pl.BlockSpec((1, CONV_WIDTH, d), lambda b, i: (a, 0, 0)),
                  vec(), vec(), vec(),
                  pl.BlockSpec((1, d, d), lambda b, i: (a, 0, 0)),
                  vec(),
                  _ln_spec(d, li, 0, 2), _ln_spec(d, li, 0, 2)],
        out_specs=pl.BlockSpec((tt, d), lambda b, i: (b * nt + i, 0)),
        out_shape=jax.ShapeDtypeStruct((t, d), F32),
        scratch_shapes=[pltpu.VMEM((CONV_HALO + tt, d), F32), pltpu.VMEM((tt, d), F32)],
        compiler_params=_params("parallel", "arbitrary"),
        name="conv_prompt",
    )(u3, u3, x, w_dw, b_dw, cn_g, cn_b, w_pw2, b_pw2, ln_g, ln_b)


def _conv_sample_kernel(u_ref, st_ref, x_ref, wdw_ref, bdw_ref, cng_ref, cnb_ref, w2_ref,
                        b2_ref, lng_ref, lnb_ref, o_ref, ns_ref, win_ref, *, alpha):
    s, nb, d = u_ref.shape
    win_ref[0:CONV_HIST] = st_ref[0]
    win_ref[CONV_HIST:CONV_HIST + s] = u_ref[...]
    ns_ref[0] = win_ref[s:s + CONV_HIST]
    acc = wdw_ref[0, 0:1, :] * win_ref[0:s]
    for j in range(1, CONV_WIDTH):
        acc = acc + wdw_ref[0, j:j + 1, :] * win_ref[j:j + s]
    out = _conv_tail(acc.reshape(s * nb, d), x_ref[...].reshape(s * nb, d), bdw_ref, cng_ref,
                     cnb_ref, w2_ref, b2_ref, lng_ref, lnb_ref, alpha)
    o_ref[...] = out.reshape(s, nb, d)


def _conv_sample(u, x, state_t, wts, a, li, alpha):
    w_dw, b_dw, cn_g, cn_b, w_pw2, b_pw2, ln_g, ln_b = wts
    t, d = x.shape
    nbatch = state_t.shape[2]
    s = t // nbatch
    nb = min(16, nbatch)
    vec = lambda: pl.BlockSpec((1, 1, d), lambda i: (a, 0, 0))
    tok = lambda: pl.BlockSpec((s, nb, d), lambda i: (0, i, 0))
    x_new, st_new = pl.pallas_call(
        functools.partial(_conv_sample_kernel, alpha=alpha),
        grid=(nbatch // nb,),
        in_specs=[tok(),
                  pl.BlockSpec((1, CONV_HIST, nb, d), lambda i: (a, 0, i, 0)),
                  tok(),
                  pl.BlockSpec((1, CONV_WIDTH, d), lambda i: (a, 0, 0)),
                  vec(), vec(), vec(),
                  pl.BlockSpec((1, d, d), lambda i: (a, 0, 0)),
                  vec(),
                  _ln_spec(d, li, 0, 1), _ln_spec(d, li, 0, 1)],
        out_specs=[tok(), pl.BlockSpec((1, CONV_HIST, nb, d), lambda i: (0, 0, i, 0))],
        out_shape=[jax.ShapeDtypeStruct((s, nbatch, d), F32),
                   jax.ShapeDtypeStruct((1, CONV_HIST, nbatch, d), F32)],
        scratch_shapes=[pltpu.VMEM((CONV_HIST + s, nb, d), F32)],
        compiler_params=_params("parallel"),
        name="conv_sample",
    )(u.reshape(s, nbatch, d), state_t, x.reshape(s, nbatch, d), w_dw, b_dw, cn_g, cn_b,
      w_pw2, b_pw2, ln_g, ln_b)
    return x_new.reshape(t, d), st_new


def _ffn_kernel(x_ref, wg_ref, wu_ref, wd_ref, lng_ref, lnb_ref, o_ref, acc_ref, *, alpha):
    k = pl.program_id(1)

    @pl.when(k == 0)
    def _():
        acc_ref[...] = jnp.zeros_like(acc_ref)

    x = x_ref[...].astype(BF16)
    g = _dot(x, wg_ref[0])
    u = _dot(x, wu_ref[0])
    h = (g * _sigmoid(g) * u).astype(BF16)
    acc_ref[...] += _dot(h, wd_ref[0])

    @pl.when(k == pl.num_programs(1) - 1)
    def _():
        o_ref[...] = _layer_norm(alpha * x_ref[...] + acc_ref[...], lng_ref[0, 0], lnb_ref[0, 0])


def _ffn(x, w_gu, w_down, ln_g, ln_b, a, li, alpha):
    t, d = x.shape
    f = w_down.shape[1]
    tm = _token_tile(t)
    tf = min(FF_TILE, f)
    nf = f // tf
    return pl.pallas_call(
        functools.partial(_ffn_kernel, alpha=alpha),
        grid=(t // tm, nf),
        in_specs=[pl.BlockSpec((tm, d), lambda i, k: (i, 0)),
                  pl.BlockSpec((1, d, tf), lambda i, k: (a, 0, k)),
                  pl.BlockSpec((1, d, tf), lambda i, k: (a, 0, k + nf)),
                  pl.BlockSpec((1, tf, d), lambda i, k: (a, k, 0)),
                  _ln_spec(d, li, 1, 2), _ln_spec(d, li, 1, 2)],
        out_specs=pl.BlockSpec((tm, d), lambda i, k: (i, 0)),
        out_shape=jax.ShapeDtypeStruct((t, d), F32),
        scratch_shapes=[pltpu.VMEM((tm, d), F32)],
        compiler_params=_params("parallel", "arbitrary"),
        name="ffn_dense",
    )(x, w_gu, w_gu, w_down, ln_g, ln_b)


def _router_kernel(x_ref, wt_ref, b_ref, c_ref, r_ref, count_ref, *, tiles_per_group):
    i = pl.program_id(0)
    x = x_ref[...]
    tr = x.shape[0]
    n_exp = wt_ref.shape[1]
    lane = lax.broadcasted_iota(jnp.int32, (tr, LANES), 1).astype(F32)
    logits = jnp.where(lane < n_exp, b_ref[0], -jnp.inf)
    for e in range(n_exp):
        col = jnp.sum(x * wt_ref[0, e:e + 1, :], axis=1, keepdims=True)
        logits = logits + jnp.where(lane == e, col, 0.0)
    m1 = jnp.max(logits, axis=1, keepdims=True)
    i1 = jnp.min(jnp.where(logits == m1, lane, float(LANES)), axis=1, keepdims=True)
    rest = jnp.where(lane == i1, -jnp.inf, logits)
    m2 = jnp.max(rest, axis=1, keepdims=True)
    i2 = jnp.min(jnp.where(rest == m2, lane, float(LANES)), axis=1, keepdims=True)
    e2 = jnp.exp(m2 - m1)
    g1 = 1.0 / (1.0 + e2)
    g2 = e2 / (1.0 + e2)
    c_ref[...] = jnp.where(lane == i1, g1, 0.0) + jnp.where(lane == i2, g2, 0.0)

    @pl.when(i % tiles_per_group == 0)
    def _():
        count_ref[...] = jnp.zeros_like(count_ref)

    member = (lane == i1) | (lane == i2)
    below = (lax.broadcasted_iota(jnp.int32, (tr, tr), 0) >
             lax.broadcasted_iota(jnp.int32, (tr, tr), 1))
    earlier = _dot(below.astype(BF16), member.astype(BF16))
    r_ref[...] = jnp.where(member, earlier + count_ref[...], -1.0)
    count_ref[...] += jnp.sum(member.astype(F32), axis=0, keepdims=True)


def _router(x, w_router_t, b_router, a, moe_tile):
    t, d = x.shape
    n_exp = w_router_t.shape[1]
    tr = min(ROUTER_TILE, t)
    b_pad = jnp.pad(b_router, ((0, 0), (0, 0), (0, LANES - n_exp)))
    return pl.pallas_call(
        functools.partial(_router_kernel, tiles_per_group=moe_tile // tr),
        grid=(t // tr,),
        in_specs=[pl.BlockSpec((tr, d), lambda i: (i, 0)),
                  pl.BlockSpec((1, n_exp, d), lambda i: (a, 0, 0)),
                  pl.BlockSpec((1, 1, LANES), lambda i: (a, 0, 0))],
        out_specs=[pl.BlockSpec((tr, LANES), lambda i: (i, 0)),
                   pl.BlockSpec((tr, LANES), lambda i: (i, 0))],
        out_shape=[jax.ShapeDtypeStruct((t, LANES), F32)] * 2,
        scratch_shapes=[pltpu.VMEM((1, LANES), F32)],
        compiler_params=_params("arbitrary"),
        name="router",
    )(x, w_router_t, b_pad)


def _slot_rows(q):
    return pl.ds(pl.multiple_of(q * MOE_SLOTS, MOE_SLOTS), MOE_SLOTS)


def _moe_kernel(pre_ref, xb_ref, rt_ref, r_ref, c_ref, wg_ref, wu_ref, wd_ref, y_ref,
                xs_ref, ys_ref, *, n_chunks, chunk):
    i, e, k = pl.program_id(0), pl.program_id(1), pl.program_id(2)
    base = (i * pl.num_programs(1) + e) * (n_chunks + 1)
    n_sel = pre_ref[base + n_chunks]
    n_blocks = (n_sel + (MOE_SLOTS - 1)) >> SLOT_SHIFT

    def chunk_blocks(c):
        s0, s1 = pre_ref[base + c], pre_ref[base + c + 1]
        lo = s0 >> SLOT_SHIFT
        hi = (s1 + (MOE_SLOTS - 1)) >> SLOT_SHIFT
        return lo, jnp.where(s1 > s0, hi, lo)

    @pl.when((e == 0) & (k == 0))
    def _():
        y_ref[...] = jnp.zeros_like(y_ref)

    @pl.when(k == 0)
    def _():
        def clear(q, carry):
            xs_ref[_slot_rows(q), :] = jnp.zeros((MOE_SLOTS, xs_ref.shape[1]), BF16)
            return carry
        lax.fori_loop(0, n_blocks, clear, 0)
        for c in range(n_chunks):
            rank_row = rt_ref[0, :, c * chunk:(c + 1) * chunk]
            xc = xb_ref[c * chunk:(c + 1) * chunk, :]
            lo, hi = chunk_blocks(c)

            def gather(q, carry, rank_row=rank_row, xc=xc):
                slot = q * MOE_SLOTS + lax.broadcasted_iota(jnp.int32, (MOE_SLOTS, chunk), 0)
                onehot = (slot.astype(F32) == rank_row).astype(BF16)
                xs_ref[_slot_rows(q), :] += _dot(onehot, xc).astype(BF16)
                return carry
            lax.fori_loop(lo, hi, gather, 0)

    def ffn(start, size):
        rows = pl.ds(pl.multiple_of(start, MOE_SLOTS), size)
        xq = xs_ref[rows, :]
        g = _dot(xq, wg_ref[0, 0])
        u = _dot(xq, wu_ref[0, 0])
        h = (g * _sigmoid(g) * u).astype(BF16)
        part = _dot(h, wd_ref[0, 0])

        @pl.when(k == 0)
        def _():
            ys_ref[rows, :] = part

        @pl.when(k > 0)
        def _():
            ys_ref[rows, :] += part

    def ffn_quad(p, carry):
        ffn(p * (4 * MOE_SLOTS), 4 * MOE_SLOTS)
        return carry
    n_quads = n_blocks >> 2
    lax.fori_loop(0, n_quads, ffn_quad, 0)

    @pl.when((n_blocks & 2) != 0)
    def _():
        ffn(n_quads * (4 * MOE_SLOTS), 2 * MOE_SLOTS)

    @pl.when((n_blocks & 1) != 0)
    def _():
        ffn((n_blocks - 1) * MOE_SLOTS, MOE_SLOTS)

    @pl.when(k == pl.num_programs(2) - 1)
    def _():
        for c in range(n_chunks):
            tok = slice(c * chunk, (c + 1) * chunk)
            lane = lax.broadcasted_iota(jnp.int32, (chunk, LANES), 1)
            rank_col = jnp.sum(jnp.where(lane == e, r_ref[tok, :], 0.0), axis=1, keepdims=True)
            gate_col = jnp.sum(jnp.where(lane == e, c_ref[tok, :], 0.0), axis=1, keepdims=True)
            lo, hi = chunk_blocks(c)

            def scatter(q, carry, tok=tok, rank_col=rank_col, gate_col=gate_col):
                slot = q * MOE_SLOTS + lax.broadcasted_iota(jnp.int32, (chunk, MOE_SLOTS), 1)
                onehot = (rank_col == slot.astype(F32)).astype(BF16)
                y_ref[tok, :] += gate_col * _dot(onehot, ys_ref[_slot_rows(q), :].astype(BF16))
                return carry
            lax.fori_loop(lo, hi, scatter, 0)


def _moe(xb, combine, rank, w_gu, w_down, a):
    t, d = xb.shape
    n_exp, f = w_down.shape[1], w_down.shape[2]
    tm = min(MOE_TOKEN_TILE, t)
    nt = t // tm
    chunk = min(MOE_CHUNK, tm)
    n_chunks = tm // chunk
    cap = max(-(-tm // MOE_SLOTS), 4) * MOE_SLOTS
    tf = MOE_FF_TILE if f % MOE_FF_TILE == 0 else min(FF_TILE, f)
    nf = f // tf
    member = (rank[:, :n_exp] >= 0).astype(jnp.int32)
    per_chunk = member.reshape(nt, n_chunks, chunk, n_exp).sum(axis=2)
    pre = jnp.concatenate([jnp.zeros((nt, 1, n_exp), jnp.int32), jnp.cumsum(per_chunk, axis=1)], axis=1)
    pre = jnp.swapaxes(pre, 1, 2).reshape(-1)
    rank_t = jnp.swapaxes(rank[:, :n_exp], 0, 1)[:, None, :]
    grid_spec = pltpu.PrefetchScalarGridSpec(
        num_scalar_prefetch=1,
        grid=(nt, n_exp, nf),
        in_specs=[pl.BlockSpec((tm, d), lambda i, e, k, pre: (i, 0)),
                  pl.BlockSpec((1, 1, tm), lambda i, e, k, pre: (e, 0, i)),
                  pl.BlockSpec((tm, LANES), lambda i, e, k, pre: (i, 0)),
                  pl.BlockSpec((tm, LANES), lambda i, e, k, pre: (i, 0)),
                  pl.BlockSpec((1, 1, d, tf), lambda i, e, k, pre: (a, e, 0, k)),
                  pl.BlockSpec((1, 1, d, tf), lambda i, e, k, pre: (a, e, 0, k + nf)),
                  pl.BlockSpec((1, 1, tf, d), lambda i, e, k, pre: (a, e, k, 0))],
        out_specs=pl.BlockSpec((tm, d), lambda i, e, k, pre: (i, 0)),
        scratch_shapes=[pltpu.VMEM((cap, d), BF16), pltpu.VMEM((cap, d), F32)])
    return pl.pallas_call(
        functools.partial(_moe_kernel, n_chunks=n_chunks, chunk=chunk),
        grid_spec=grid_spec,
        out_shape=jax.ShapeDtypeStruct((t, d), F32),
        compiler_params=_params("parallel", "arbitrary", "arbitrary"),
        name="moe_sparse",
    )(pre, xb, rank_t, rank, combine, w_gu, w_gu, w_down)


def _residual_ln_kernel(x_ref, y_ref, lng_ref, lnb_ref, o_ref, *, alpha):
    o_ref[...] = _layer_norm(alpha * x_ref[...] + y_ref[...], lng_ref[0, 0], lnb_ref[0, 0])


def _residual_ln(x, y, ln_g, ln_b, li, alpha):
    t, d = x.shape
    tm = _token_tile(t)
    tok = lambda: pl.BlockSpec((tm, d), lambda i: (i, 0))
    return pl.pallas_call(
        functools.partial(_residual_ln_kernel, alpha=alpha),
        grid=(t // tm,),
        in_specs=[tok(), tok(), _ln_spec(d, li, 1, 1), _ln_spec(d, li, 1, 1)],
        out_specs=tok(),
        out_shape=jax.ShapeDtypeStruct((t, d), F32),
        compiler_params=_params("parallel"),
        name="residual_ln",
    )(x, y, ln_g, ln_b)


def _qkv_kernel(x_ref, w_ref, cos_ref, sin_ref, *refs, dils):
    o_refs, z_ref = refs[:N_GROUPS], refs[N_GROUPS]
    j = pl.program_id(1)
    g = j // 3
    kind = j - 3 * g
    z = _dot(x_ref[...].astype(BF16), w_ref[0])
    tm, width = z.shape

    @pl.when(kind < 2)
    def _():
        cos, sin = cos_ref[...], sin_ref[...]
        lane = lax.broadcasted_iota(jnp.int32, cos.shape, 1)
        half = HEAD_DIM // 2
        first = (lane & (HEAD_DIM - 1)) < half
        scale = jnp.where(kind == 0, HEAD_DIM ** -0.5, 1.0)
        for t in range(width // LANES):
            zt = z[:, t * LANES:(t + 1) * LANES]
            partner = jnp.where(first, pltpu.roll(zt, LANES - half, 1), pltpu.roll(zt, half, 1))
            z_ref[t] = (zt * cos + partner * sin) * scale

    @pl.when(kind == 2)
    def _():
        for t in range(width // LANES):
            z_ref[t] = z[:, t * LANES:(t + 1) * LANES]

    for gi, dil in enumerate(dils):
        @pl.when(g == gi)
        def _(gi=gi, dil=dil):
            for t in range(width // LANES):
                lanes = slice(t * LANES, (t + 1) * LANES)
                if dil == 1:
                    o_refs[gi][0, :, lanes] = z_ref[t]
                else:
                    for c in range(dil):
                        o_refs[gi][c, :, lanes] = z_ref[t, pl.ds(c, tm // dil, stride=dil), :]


def _rope_tables(pos):
    half = HEAD_DIM // 2
    inv = ROPE_THETA ** (-jnp.arange(half, dtype=F32) / half)
    ang = pos.astype(F32)[:, None] * inv[None, :]
    cos, sin = jnp.cos(ang), jnp.sin(ang)
    cos2 = jnp.concatenate([cos, cos], axis=1)
    sin2 = jnp.concatenate([-sin, sin], axis=1)
    return jnp.concatenate([cos2, cos2], axis=1), jnp.concatenate([sin2, sin2], axis=1)


def _qkv(x, w_qkv, pos, a, nseq, dils):
    t, d = x.shape
    seq = t // nseq
    tm = _token_tile(seq)
    tiles = seq // tm
    cos, sin = _rope_tables(pos)

    def out_spec(gi, dil):
        return pl.BlockSpec(
            (None, dil, tm // dil, GROUP_WIDTH),
            lambda i, j: (i // tiles, 0, i % tiles, jnp.clip(j - 3 * gi, 0, 2)))

    def w_col(i, j):
        return (a, 0, (j % 3) * N_GROUPS + j // 3)

    return pl.pallas_call(
        functools.partial(_qkv_kernel, dils=dils),
        grid=(t // tm, 3 * N_GROUPS),
        in_specs=[pl.BlockSpec((tm, d), lambda i, j: (i, 0)),
                  pl.BlockSpec((1, d, GROUP_WIDTH), w_col),
                  pl.BlockSpec((tm, 2 * HEAD_DIM), lambda i, j: (i, 0)),
                  pl.BlockSpec((tm, 2 * HEAD_DIM), lambda i, j: (i, 0))],
        out_specs=[out_spec(gi, dil) for gi, dil in enumerate(dils)],
        out_shape=[jax.ShapeDtypeStruct((nseq, dil, seq // dil, 3 * GROUP_WIDTH), F32)
                   for dil in dils],
        scratch_shapes=[pltpu.VMEM((GROUP_WIDTH // LANES, tm, LANES), F32)],
        compiler_params=_params("parallel", "arbitrary"),
        name="qkv_rope",
    )(x, w_qkv, cos, sin)


def _band_attn_kernel(q_ref, kp_ref, kc_ref, vp_ref, vc_ref, o_ref, l_ref, *, dil):
    n, c = pl.program_id(1), pl.program_id(2)
    tq = q_ref.shape[0]
    qi = lax.broadcasted_iota(jnp.int32, (tq, 2 * tq), 0) + tq
    ki = lax.broadcasted_iota(jnp.int32, (tq, 2 * tq), 1)
    dist = qi - ki
    mask = (dist >= 0) & (dist <= tq) & ((n > 0) | (ki >= tq))
    q = q_ref[...]
    k = jnp.concatenate([kp_ref[...], kc_ref[...]], axis=0).astype(BF16)
    v = jnp.concatenate([vp_ref[...], vc_ref[...]], axis=0).astype(BF16)
    lower = lax.broadcasted_iota(jnp.int32, (tq, LANES), 1) < HEAD_DIM
    outs, lses = [], []
    for t in range(GROUP_WIDTH // LANES):
        sl = slice(t * LANES, (t + 1) * LANES)
        o_t = lse_t = None
        for own in (lower, ~lower):
            qm = jnp.where(own, q[:, sl], 0.0).astype(BF16)
            s = jnp.where(mask, _dot_nt(qm, k[:, sl]), NEG_INF)
            m = jnp.max(s, axis=1, keepdims=True)
            p = jnp.exp(s - m)
            l = jnp.sum(p, axis=1, keepdims=True)
            o_h = _dot(p.astype(BF16), v[:, sl]) / l
            lse_h = jnp.broadcast_to(m + jnp.log(l), (tq, LANES))
            o_t = o_h if o_t is None else jnp.where(lower, o_t, o_h)
            lse_t = lse_h if lse_t is None else jnp.where(lower, lse_t, lse_h)
        outs.append(o_t)
        lses.append(lse_t)
    if dil == 1:
        for t in range(len(outs)):
            o_ref[t] = outs[t]
            l_ref[t] = lses[t]
    else:
        for cc in range(dil):
            @pl.when(c == cc)
            def _(cc=cc):
                for t in range(len(outs)):
                    o_ref[t, pl.ds(cc, tq, stride=dil), :] = outs[t]
                    l_ref[t, pl.ds(cc, tq, stride=dil), :] = lses[t]


def _band_attn(qkv_g, g):
    win, dil = GROUPS[g]
    assert win // dil == BAND
    batch, _, lc, _ = qkv_g.shape
    assert lc % BAND == 0
    nblk = lc // BAND
    tiles = GROUP_WIDTH // LANES
    prev = lambda n: jnp.maximum(n - 1, 0)
    blk = (None, None, BAND, GROUP_WIDTH)
    out = pl.BlockSpec((tiles, BAND * dil, LANES), lambda b, n, c: (0, b * nblk + n, 0))
    return pl.pallas_call(
        functools.partial(_band_attn_kernel, dil=dil),
        grid=(batch, nblk, dil),
        in_specs=[pl.BlockSpec(blk, lambda b, n, c: (b, c, n, 0)),
                  pl.BlockSpec(blk, lambda b, n, c: (b, c, prev(n), 1)),
                  pl.BlockSpec(blk, lambda b, n, c: (b, c, n, 1)),
                  pl.BlockSpec(blk, lambda b, n, c: (b, c, prev(n), 2)),
                  pl.BlockSpec(blk, lambda b, n, c: (b, c, n, 2))],
        out_specs=[out, out],
        out_shape=[jax.ShapeDtypeStruct((tiles, batch * lc * dil, LANES), F32)] * 2,
        compiler_params=_params("parallel", "arbitrary", "arbitrary"),
        name=f"band_attn_g{g}",
    )(qkv_g, qkv_g, qkv_g, qkv_g, qkv_g)


def _sample_attn_kernel(q_ref, kn_ref, vn_ref, k_ref, v_ref, *refs, dil, steps):
    ko_ref, vo_ref, o_ref, l_ref = refs[-4:]
    win = k_ref.shape[-1]
    last = win - LANES
    lane_w = lax.broadcasted_iota(jnp.int32, (steps, win), 1)
    lane_f = lax.broadcasted_iota(jnp.int32, (steps, LANES), 1)
    step_w = lax.broadcasted_iota(jnp.int32, (steps, win), 0)
    step_f = lax.broadcasted_iota(jnp.int32, (steps, LANES), 0)
    row_w = lane_w + steps
    mask_w = (((row_w - step_w) & (dil - 1)) == 0) & (row_w <= win + step_w)
    mask_f = (lane_f < steps) & (lane_f >= step_f) & (((lane_f - step_f) & (dil - 1)) == 0)
    new_lane = lax.broadcasted_iota(jnp.int32, (HEAD_DIM, LANES), 1) >= LANES - steps

    def shifted(old, new):
        moved = pltpu.roll(old, win - steps, 1)
        tail = jnp.where(new_lane, new, moved[:, last:])
        return tail if last == 0 else jnp.concatenate([moved[:, :last], tail], axis=1)

    rows = k_ref.shape[0]
    group = rows // kn_ref.shape[0]

    def one_row(i, u):
        r = i * group + u
        to_tail = (LANES - steps - u * steps) % LANES
        place = lambda t: t if to_tail == 0 else pltpu.roll(t, to_tail, 1)
        k_old, v_old = k_ref[r], v_ref[r]
        k_new = shifted(k_old, place(kn_ref[i]))
        v_new = shifted(v_old, place(vn_ref[i]))
        ko_ref[r] = k_new
        vo_ref[r] = v_new
        q = q_ref[r].astype(BF16)
        s_w = jnp.where(mask_w, _dot(q, k_new.astype(BF16)), NEG_INF)
        s_f = jnp.where(mask_f, _dot(q, k_old[:, :LANES].astype(BF16)), NEG_INF)
        m = jnp.maximum(jnp.max(s_w, axis=1, keepdims=True), jnp.max(s_f, axis=1, keepdims=True))
        p_w = jnp.exp(s_w - m)
        p_f = jnp.exp(s_f - m)
        l = jnp.sum(p_w, axis=1, keepdims=True) + jnp.sum(p_f, axis=1, keepdims=True)
        o = (_dot_nt(p_w.astype(BF16), v_new.astype(BF16)) +
             _dot_nt(p_f.astype(BF16), v_old[:, :LANES].astype(BF16)))
        o_ref[r] = o / l
        l_ref[r] = jnp.broadcast_to(m + jnp.log(l), (steps, HEAD_DIM))

    def body(i, carry):
        for u in range(group):
            one_row(i, u)
        return carry
    lax.fori_loop(0, rows // group, body, 0)


def _cache_rows_per_step(rows, win):
    return max(1, min(rows, CACHE_BLOCK_BYTES // (HEAD_DIM * win * 4)))


def _sample_attn(q, k_new_t, v_new_t, cache_kt, cache_vt, prev_out, a, g):
    win, dil = GROUPS[g]
    assert win // dil == BAND
    layers, rows, _, w = cache_kt.shape
    assert w == win
    steps = q.shape[1]
    rb = _cache_rows_per_step(rows, win)
    pack = rows // k_new_t.shape[0]
    assert rb % pack == 0
    cache = lambda: pl.BlockSpec((None, rb, HEAD_DIM, win), lambda i: (a, i, 0, 0))
    tok = lambda: pl.BlockSpec((rb, steps, HEAD_DIM), lambda i: (i, 0, 0))
    new = lambda: pl.BlockSpec((rb // pack, HEAD_DIM, LANES), lambda i: (i, 0, 0))
    in_specs = [tok(), new(), new(), cache(), cache()]
    args = [q, k_new_t, v_new_t, cache_kt, cache_vt]
    aliases = {}
    if prev_out is not None:
        in_specs += [pl.BlockSpec(memory_space=pl.ANY)] * 2
        args += list(prev_out)
        aliases = {5: 0, 6: 1}
    return pl.pallas_call(
        functools.partial(_sample_attn_kernel, dil=dil, steps=steps),
        grid=(rows // rb,),
        in_specs=in_specs,
        out_specs=[cache(), cache(), tok(), tok()],
        out_shape=[jax.ShapeDtypeStruct(cache_kt.shape, F32)] * 2 +
                  [jax.ShapeDtypeStruct(q.shape, F32)] * 2,
        input_output_aliases=aliases,
        compiler_params=_params("parallel"),
        name=f"sample_attn_g{g}",
    )(*args)


def _merge_kernel(*refs, alpha):
    o_refs = refs[0:N_GROUPS]
    l_refs = refs[N_GROUPS:2 * N_GROUPS]
    x_ref, wo_ref, lng_ref, lnb_ref, out_ref, outb_ref = refs[2 * N_GROUPS:]
    lses = [r[...] for r in l_refs]
    top = functools.reduce(jnp.maximum, lses)
    es = [jnp.exp(l - top) for l in lses]
    den = functools.reduce(jnp.add, es)
    y = functools.reduce(jnp.add, [(e / den) * r[...] for e, r in zip(es, o_refs)])
    y = jnp.concatenate([y[t] for t in range(y.shape[0])], axis=1)
    m = _dot(y.astype(BF16), wo_ref[0])
    out = _layer_norm(alpha * x_ref[...] + m, lng_ref[0, 0], lnb_ref[0, 0])
    out_ref[...] = out
    outb_ref[...] = out.astype(BF16)


def _merge(outs, lses, x, w_o, ln_g, ln_b, a, li, alpha):
    t, d = x.shape
    tm = min(512, t)
    grp = lambda: pl.BlockSpec((GROUP_WIDTH // LANES, tm, LANES), lambda i: (0, i, 0))
    tok = lambda: pl.BlockSpec((tm, d), lambda i: (i, 0))
    return pl.pallas_call(
        functools.partial(_merge_kernel, alpha=alpha),
        grid=(t // tm,),
        in_specs=[grp() for _ in range(2 * N_GROUPS)] + [
            tok(), pl.BlockSpec((1, GROUP_WIDTH, d), lambda i: (a, 0, 0)),
            _ln_spec(d, li, 0, 1), _ln_spec(d, li, 0, 1)],
        out_specs=[tok(), tok()],
        out_shape=[jax.ShapeDtypeStruct((t, d), F32), jax.ShapeDtypeStruct((t, d), BF16)],
        compiler_params=_params("parallel"),
        name="merge_wo",
    )(*outs, *lses, x, w_o, ln_g, ln_b)


def _heads_first(x, nbatch, steps):
    x = x.reshape(steps, nbatch, HEADS, HEAD_DIM)
    return jnp.transpose(x, (1, 2, 0, 3)).reshape(nbatch * HEADS, steps, HEAD_DIM)


def _new_columns(x, nbatch, steps, pack):
    x = x.reshape(steps, nbatch * HEADS // pack, pack, HEAD_DIM)
    x = jnp.transpose(x, (1, 3, 2, 0)).reshape(nbatch * HEADS // pack, HEAD_DIM, pack * steps)
    return jnp.pad(x, ((0, 0), (0, 0), (0, LANES - pack * steps)))


def _tokens_first(x, nbatch, steps):
    x = x.reshape(nbatch, HEADS // 2, 2, steps, HEAD_DIM)
    return jnp.transpose(x, (1, 3, 0, 2, 4)).reshape(HEADS // 2, steps * nbatch, LANES)


def kernel(x_prompt, x_sample, state_conv, cache_k_w128, cache_v_w128, cache_k_w512, cache_v_w512, cache_k_w2048, cache_v_w2048, ln_g, ln_b, w_pw1, b_pw1, w_dw, b_dw, cn_g, cn_b, w_pw2, b_pw2, w_qkv, w_o, w_ffn_gu, w_ffn_down, w_router, b_router, w_exp_gu, w_exp_down):
    batch, seq, d = x_prompt.shape
    nbatch, s_new, _ = x_sample.shape
    depth = ln_g.shape[0]
    alpha = (2 * depth) ** 0.25
    caches = ((cache_k_w128, cache_v_w128), (cache_k_w512, cache_v_w512),
              (cache_k_w2048, cache_v_w2048))
    caches_t = [[jnp.transpose(c, (0, 1, 3, 4, 2)).reshape(c.shape[0], nbatch * HEADS, HEAD_DIM, c.shape[2])
                 for c in kv] for kv in caches]

    w_pw1_h, w_pw2_h, w_qkv_h, w_o_h = (w.astype(BF16) for w in (w_pw1, w_pw2, w_qkv, w_o))
    w_ffn_gu_h, w_ffn_down_h = w_ffn_gu.astype(BF16), w_ffn_down.astype(BF16)
    w_exp_gu_h, w_exp_down_h = w_exp_gu.astype(BF16), w_exp_down.astype(BF16)
    row = lambda v: v[:, None, :]
    conv_wts = (w_dw, row(b_dw), row(cn_g), row(cn_b), w_pw2_h, row(b_pw2))
    ln_g4, ln_b4 = ln_g[:, :, None, :], ln_b[:, :, None, :]
    w_router_t = jnp.swapaxes(w_router, 1, 2)
    b_router3 = row(b_router)
    state_t = jnp.transpose(state_conv, (0, 2, 1, 3))

    pos_p = jnp.tile(jnp.arange(seq), batch)
    pos_s = jnp.repeat(PAST_LEN + jnp.arange(s_new), nbatch)
    dils = tuple(dil for _, dil in GROUPS)

    xp = x_prompt.reshape(batch * seq, d)
    xs = jnp.transpose(x_sample, (1, 0, 2)).reshape(s_new * nbatch, d)
    conv_p, conv_s = [], []
    kv_p = [[] for _ in range(2 * N_GROUPS)]
    kv_s = [None] * N_GROUPS
    for li in range(depth):
        a = li // 2
        if li % 2 == 0:
            wts = conv_wts + (ln_g4, ln_b4)
            up = _pw1_glu(xp, w_pw1_h, row(b_pw1), a)
            us = _pw1_glu(xs, w_pw1_h, row(b_pw1), a)
            xp = _conv_prompt(up, xp, batch, wts, a, li, alpha)
            xs, st = _conv_sample(us, xs, state_t, wts, a, li, alpha)
            conv_p.append(up.reshape(batch, seq, d)[:, seq - CONV_HIST:])
            conv_s.append(st)
            xp = _ffn(xp, w_ffn_gu_h, w_ffn_down_h, ln_g4, ln_b4, a, li, alpha)
            xs = _ffn(xs, w_ffn_gu_h, w_ffn_down_h, ln_g4, ln_b4, a, li, alpha)
        else:
            qkv_p = _qkv(xp, w_qkv_h, pos_p, a, batch, dils)
            qkv_s = _qkv(xs, w_qkv_h, pos_s, a, 1, (1,) * N_GROUPS)
            outs_p, lses_p, outs_s, lses_s = [], [], [], []
            for g, (win, dil) in enumerate(GROUPS):
                o, l = _band_attn(qkv_p[g], g)
                outs_p.append(o)
                lses_p.append(l)
                lc = seq // dil
                for j in range(2):
                    tail = qkv_p[g][:, :, lc - BAND:, (1 + j) * GROUP_WIDTH:(2 + j) * GROUP_WIDTH]
                    tail = jnp.swapaxes(tail, 1, 2).reshape(batch, win, HEADS, HEAD_DIM)
                    kv_p[2 * g + j].append(tail)
                part = lambda j: qkv_s[g][0, 0, :, j * GROUP_WIDTH:(j + 1) * GROUP_WIDTH]
                pack = min(_cache_rows_per_step(nbatch * HEADS, win), LANES // s_new)
                res = _sample_attn(_heads_first(part(0), nbatch, s_new),
                                   _new_columns(part(1), nbatch, s_new, pack),
                                   _new_columns(part(2), nbatch, s_new, pack),
                                   caches_t[g][0], caches_t[g][1], kv_s[g], a, g)
                kv_s[g] = res[:2]
                outs_s.append(_tokens_first(res[2], nbatch, s_new))
                lses_s.append(_tokens_first(res[3], nbatch, s_new))
            xp, xpb = _merge(outs_p, lses_p, xp, w_o_h, ln_g4, ln_b4, a, li, alpha)
            xs, xsb = _merge(outs_s, lses_s, xs, w_o_h, ln_g4, ln_b4, a, li, alpha)
            for stream in range(2):
                x, xb = (xp, xpb) if stream == 0 else (xs, xsb)
                combine, rank = _router(x, w_router_t, b_router3, a, min(MOE_TOKEN_TILE, x.shape[0]))
                y = _moe(xb, combine, rank, w_exp_gu_h, w_exp_down_h, a)
                x = _residual_ln(x, y, ln_g4, ln_b4, li, alpha)
                if stream == 0:
                    xp = x
                else:
                    xs = x

    kv_p_out = [jnp.stack(t) for t in kv_p]
    kv_s_out = []
    for g, (win, _) in enumerate(GROUPS):
        for j in range(2):
            ct = kv_s[g][j].reshape(-1, nbatch, HEADS, HEAD_DIM, win)
            kv_s_out.append(jnp.transpose(ct, (0, 1, 4, 2, 3)))
    conv_s_out = jnp.transpose(jnp.concatenate(conv_s, axis=0), (0, 2, 1, 3))
    y_sample = jnp.transpose(xs.reshape(s_new, nbatch, d), (1, 0, 2))
    return (xp.reshape(batch, seq, d), y_sample, jnp.stack(conv_p), conv_s_out,
            *kv_p_out, *kv_s_out)
```

```python
import functools

import jax
import jax.numpy as jnp
from jax import lax
from jax.experimental import pallas as pl
from jax.experimental.pallas import tpu as pltpu

GROUPS = ((128, 1), (512, 4), (2048, 16))
N_GROUPS = len(GROUPS)
HEADS = 8
HEAD_DIM = 64
GROUP_WIDTH = HEADS * HEAD_DIM
BAND = 128
BAND_BLOCKS_PER_STEP = 2
CONV_WIDTH = 31
CONV_HIST = CONV_WIDTH - 1
CONV_HALO = 32
PAST_LEN = 2048
ROPE_THETA = 10000.0
LN_EPS = 1e-5
NEG_INF = -1e30
LANES = 128

VMEM_LIMIT_BYTES = 56 * 1024 * 1024
TOKEN_TILE = 1024
FF_TILE = 512
MOE_TOKEN_TILE = 2048
MOE_FF_TILE = 896
MOE_CHUNK = 256
MOE_SLOTS = 128
SLOT_SHIFT = MOE_SLOTS.bit_length() - 1
ROUTER_TILE = 512
CACHE_BLOCK_BYTES = 2 * 1024 * 1024

F32 = jnp.float32
BF16 = jnp.bfloat16


def _params(*semantics):
    return pltpu.CompilerParams(dimension_semantics=semantics,
                                vmem_limit_bytes=VMEM_LIMIT_BYTES)


def _layer_norm(x, g, b):
    mu = jnp.mean(x, axis=-1, keepdims=True)
    xc = x - mu
    var = jnp.mean(xc * xc, axis=-1, keepdims=True)
    return xc * lax.rsqrt(var + LN_EPS) * g + b


def _sigmoid(x):
    return 1.0 / (1.0 + jnp.exp(-x))


def _dot(a, b):
    return jnp.dot(a, b, preferred_element_type=F32)


def _dot_nt(a, b):
    return lax.dot_general(a, b, (((1,), (1,)), ((), ())), preferred_element_type=F32)


def _token_tile(t):
    return min(TOKEN_TILE, t)


def _ln_spec(d, li, which, nargs):
    idx = (li, which, 0, 0)
    return pl.BlockSpec((1, 1, 1, d), {1: lambda i: idx, 2: lambda i, j: idx,
                                       3: lambda i, j, k: idx}[nargs])


def _pw1_glu_kernel(x_ref, w_ref, b_ref, u_ref):
    d = u_ref.shape[-1]
    x = x_ref[...].astype(BF16)
    a = _dot(x, w_ref[0, :, :d]) + b_ref[0, :, :d]
    g = _dot(x, w_ref[0, :, d:]) + b_ref[0, :, d:]
    u_ref[...] = a * _sigmoid(g)


def _pw1_glu(x, w_pw1, b_pw1, a):
    t, d = x.shape
    tm = _token_tile(t)
    return pl.pallas_call(
        _pw1_glu_kernel,
        grid=(t // tm,),
        in_specs=[pl.BlockSpec((tm, d), lambda i: (i, 0)),
                  pl.BlockSpec((1, d, 2 * d), lambda i: (a, 0, 0)),
                  pl.BlockSpec((1, 1, 2 * d), lambda i: (a, 0, 0))],
        out_specs=pl.BlockSpec((tm, d), lambda i: (i, 0)),
        out_shape=jax.ShapeDtypeStruct((t, d), F32),
        compiler_params=_params("parallel"),
        name="pw1_glu",
    )(x, w_pw1, b_pw1)


def _conv_tail(y, x, bdw_ref, cng_ref, cnb_ref, w2_ref, b2_ref, lng_ref, lnb_ref, alpha):
    y = y + bdw_ref[0]
    y = _layer_norm(y, cng_ref[0], cnb_ref[0])
    y = y * _sigmoid(y)
    m = _dot(y.astype(BF16), w2_ref[0]) + b2_ref[0]
    return _layer_norm(alpha * x + m, lng_ref[0, 0], lnb_ref[0, 0])


CONV_ROW_CHUNK = 32
CONV_LANE_CHUNK = 256
SUBLANES = 8


def _conv_prompt_kernel(um_ref, uh_ref, x_ref, wdw_ref, bdw_ref, cng_ref, cnb_ref, w2_ref,
                        b2_ref, lng_ref, lnb_ref, o_ref, win_ref, y_ref, *, alpha):
    i = pl.program_id(1)
    tt = um_ref.shape[1]
    win_ref[0:CONV_HALO, :] = jnp.where(i > 0, uh_ref[0], 0.0)
    win_ref[CONV_HALO:CONV_HALO + tt, :] = um_ref[0]
    lead = CONV_HALO - CONV_HIST
    span = 2 * CONV_ROW_CHUNK
    assert lead + CONV_WIDTH - 1 + CONV_ROW_CHUNK <= span
    for lc in range(0, um_ref.shape[2], CONV_LANE_CHUNK):
        lanes = slice(lc, lc + CONV_LANE_CHUNK)
        w = wdw_ref[0, :, lanes]
        for r in range(0, tt, CONV_ROW_CHUNK):
            slab = win_ref[r:r + span, lanes]
            acc = None
            for res in range(SUBLANES):
                rolled = slab if res == 0 else pltpu.roll(slab, span - res, 0)
                for j in range(CONV_WIDTH):
                    if (lead + j) % SUBLANES == res:
                        base = lead + j - res
                        term = w[j:j + 1] * rolled[base:base + CONV_ROW_CHUNK]
                        acc = term if acc is None else acc + term
            y_ref[r:r + CONV_ROW_CHUNK, lanes] = acc
    o_ref[...] = _conv_tail(y_ref[...], x_ref[...], bdw_ref, cng_ref, cnb_ref, w2_ref, b2_ref,
                            lng_ref, lnb_ref, alpha)


def _conv_prompt(u, x, batch, wts, a, li, alpha):
    w_dw, b_dw, cn_g, cn_b, w_pw2, b_pw2, ln_g, ln_b = wts
    t, d = x.shape
    seq = t // batch
    tt = min(512, seq)
    nt = seq // tt
    hb = tt // CONV_HALO
    u3 = u.reshape(batch, seq, d)
    vec = lambda: pl.BlockSpec((1, 1, d), lambda b, i: (a, 0, 0))
    return pl.pallas_call(
        functools.partial(_conv_prompt_kernel, alpha=alpha),
        grid=(batch, nt),
        in_specs=[pl.BlockSpec((1, tt, d), lambda b, i: (b, i, 0)),
                  pl.BlockSpec((1, CONV_HALO, d), lambda b, i: (b, jnp.maximum(i * hb - 1, 0), 0)),
                  pl.BlockSpec((tt, d), lambda b, i: (b * nt + i, 0)),
                  pl.BlockSpec((1, CONV_WIDTH, d), lambda b, i: (a, 0, 0)),
                  vec(), vec(), vec(),
                  pl.BlockSpec((1, d, d), lambda b, i: (a, 0, 0)),
                  vec(),
                  _ln_spec(d, li, 0, 2), _ln_spec(d, li, 0, 2)],
        out_specs=pl.BlockSpec((tt, d), lambda b, i: (b * nt + i, 0)),
        out_shape=jax.ShapeDtypeStruct((t, d), F32),
        scratch_shapes=[pltpu.VMEM((CONV_HALO + tt, d), F32), pltpu.VMEM((tt, d), F32)],
        compiler_params=_params("parallel", "arbitrary"),
        name="conv_prompt",
    )(u3, u3, x, w_dw, b_dw, cn_g, cn_b, w_pw2, b_pw2, ln_g, ln_b)


def _conv_sample_kernel(u_ref, st_ref, x_ref, wdw_ref, bdw_ref, cng_ref, cnb_ref, w2_ref,
                        b2_ref, lng_ref, lnb_ref, o_ref, ns_ref, win_ref, *, alpha):
    s, nb, d = u_ref.shape
    win_ref[0:CONV_HIST] = st_ref[0]
    win_ref[CONV_HIST:CONV_HIST + s] = u_ref[...]
    ns_ref[0] = win_ref[s:s + CONV_HIST]
    acc = wdw_ref[0, 0:1, :] * win_ref[0:s]
    for j in range(1, CONV_WIDTH):
        acc = acc + wdw_ref[0, j:j + 1, :] * win_ref[j:j + s]
    out = _conv_tail(acc.reshape(s * nb, d), x_ref[...].reshape(s * nb, d), bdw_ref, cng_ref,
                     cnb_ref, w2_ref, b2_ref, lng_ref, lnb_ref, alpha)
    o_ref[...] = out.reshape(s, nb, d)


def _conv_sample(u, x, state_t, wts, a, li, alpha):
    w_dw, b_dw, cn_g, cn_b, w_pw2, b_pw2, ln_g, ln_b = wts
    t, d = x.shape
    nbatch = state_t.shape[2]
    s = t // nbatch
    nb = min(16, nbatch)
    vec = lambda: pl.BlockSpec((1, 1, d), lambda i: (a, 0, 0))
    tok = lambda: pl.BlockSpec((s, nb, d), lambda i: (0, i, 0))
    x_new, st_new = pl.pallas_call(
        functools.partial(_conv_sample_kernel, alpha=alpha),
        grid=(nbatch // nb,),
        in_specs=[tok(),
                  pl.BlockSpec((1, CONV_HIST, nb, d), lambda i: (a, 0, i, 0)),
                  tok(),
                  pl.BlockSpec((1, CONV_WIDTH, d), lambda i: (a, 0, 0)),
                  vec(), vec(), vec(),
                  pl.BlockSpec((1, d, d), lambda i: (a, 0, 0)),
                  vec(),
                  _ln_spec(d, li, 0, 1), _ln_spec(d, li, 0, 1)],
        out_specs=[tok(), pl.BlockSpec((1, CONV_HIST, nb, d), lambda i: (0, 0, i, 0))],
        out_shape=[jax.ShapeDtypeStruct((s, nbatch, d), F32),
                   jax.ShapeDtypeStruct((1, CONV_HIST, nbatch, d), F32)],
        scratch_shapes=[pltpu.VMEM((CONV_HIST + s, nb, d), F32)],
        compiler_params=_params("parallel"),
        name="conv_sample",
    )(u.reshape(s, nbatch, d), state_t, x.reshape(s, nbatch, d), w_dw, b_dw, cn_g, cn_b,
      w_pw2, b_pw2, ln_g, ln_b)
    return x_new.reshape(t, d), st_new


def _ffn_kernel(x_ref, wg_ref, wu_ref, wd_ref, lng_ref, lnb_ref, o_ref, acc_ref, *, alpha):
    k = pl.program_id(1)

    @pl.when(k == 0)
    def _():
        acc_ref[...] = jnp.zeros_like(acc_ref)

    x = x_ref[...].astype(BF16)
    g = _dot(x, wg_ref[0])
    u = _dot(x, wu_ref[0])
    h = (g * _sigmoid(g) * u).astype(BF16)
    acc_ref[...] += _dot(h, wd_ref[0])

    @pl.when(k == pl.num_programs(1) - 1)
    def _():
        o_ref[...] = _layer_norm(alpha * x_ref[...] + acc_ref[...], lng_ref[0, 0], lnb_ref[0, 0])


def _ffn(x, w_gu, w_down, ln_g, ln_b, a, li, alpha):
    t, d = x.shape
    f = w_down.shape[1]
    tm = _token_tile(t)
    tf = min(FF_TILE, f)
    nf = f // tf
    return pl.pallas_call(
        functools.partial(_ffn_kernel, alpha=alpha),
        grid=(t // tm, nf),
        in_specs=[pl.BlockSpec((tm, d), lambda i, k: (i, 0)),
                  pl.BlockSpec((1, d, tf), lambda i, k: (a, 0, k)),
                  pl.BlockSpec((1, d, tf), lambda i, k: (a, 0, k + nf)),
                  pl.BlockSpec((1, tf, d), lambda i, k: (a, k, 0)),
                  _ln_spec(d, li, 1, 2), _ln_spec(d, li, 1, 2)],
        out_specs=pl.BlockSpec((tm, d), lambda i, k: (i, 0)),
        out_shape=jax.ShapeDtypeStruct((t, d), F32),
        scratch_shapes=[pltpu.VMEM((tm, d), F32)],
        compiler_params=_params("parallel", "arbitrary"),
        name="ffn_dense",
    )(x, w_gu, w_gu, w_down, ln_g, ln_b)


def _router_kernel(x_ref, wt_ref, b_ref, c_ref, r_ref, count_ref, *, tiles_per_group):
    i = pl.program_id(0)
    x = x_ref[...]
    tr = x.shape[0]
    n_exp = wt_ref.shape[1]
    lane = lax.broadcasted_iota(jnp.int32, (tr, LANES), 1).astype(F32)
    logits = jnp.where(lane < n_exp, b_ref[0], -jnp.inf)
    for e in range(n_exp):
        col = jnp.sum(x * wt_ref[0, e:e + 1, :], axis=1, keepdims=True)
        logits = logits + jnp.where(lane == e, col, 0.0)
    m1 = jnp.max(logits, axis=1, keepdims=True)
    i1 = jnp.min(jnp.where(logits == m1, lane, float(LANES)), axis=1, keepdims=True)
    rest = jnp.where(lane == i1, -jnp.inf, logits)
    m2 = jnp.max(rest, axis=1, keepdims=True)
    i2 = jnp.min(jnp.where(rest == m2, lane, float(LANES)), axis=1, keepdims=True)
    e2 = jnp.exp(m2 - m1)
    g1 = 1.0 / (1.0 + e2)
    g2 = e2 / (1.0 + e2)
    c_ref[...] = jnp.where(lane == i1, g1, 0.0) + jnp.where(lane == i2, g2, 0.0)

    @pl.when(i % tiles_per_group == 0)
    def _():
        count_ref[...] = jnp.zeros_like(count_ref)

    member = (lane == i1) | (lane == i2)
    below = (lax.broadcasted_iota(jnp.int32, (tr, tr), 0) >
             lax.broadcasted_iota(jnp.int32, (tr, tr), 1))
    earlier = _dot(below.astype(BF16), member.astype(BF16))
    r_ref[...] = jnp.where(member, earlier + count_ref[...], -1.0)
    count_ref[...] += jnp.sum(member.astype(F32), axis=0, keepdims=True)


def _router(x, w_router_t, b_router, a, moe_tile):
    t, d = x.shape
    n_exp = w_router_t.shape[1]
    tr = min(ROUTER_TILE, t)
    b_pad = jnp.pad(b_router, ((0, 0), (0, 0), (0, LANES - n_exp)))
    return pl.pallas_call(
        functools.partial(_router_kernel, tiles_per_group=moe_tile // tr),
        grid=(t // tr,),
        in_specs=[pl.BlockSpec((tr, d), lambda i: (i, 0)),
                  pl.BlockSpec((1, n_exp, d), lambda i: (a, 0, 0)),
                  pl.BlockSpec((1, 1, LANES), lambda i: (a, 0, 0))],
        out_specs=[pl.BlockSpec((tr, LANES), lambda i: (i, 0)),
                   pl.BlockSpec((tr, LANES), lambda i: (i, 0))],
        out_shape=[jax.ShapeDtypeStruct((t, LANES), F32)] * 2,
        scratch_shapes=[pltpu.VMEM((1, LANES), F32)],
        compiler_params=_params("arbitrary"),
        name="router",
    )(x, w_router_t, b_pad)


def _slot_rows(q):
    return pl.ds(pl.multiple_of(q * MOE_SLOTS, MOE_SLOTS), MOE_SLOTS)


def _moe_kernel(pre_ref, xb_ref, rt_ref, r_ref, c_ref, wg_ref, wu_ref, wd_ref, y_ref,
                xs_ref, ys_ref, *, n_chunks, chunk):
    i, e, k = pl.program_id(0), pl.program_id(1), pl.program_id(2)
    base = (i * pl.num_programs(1) + e) * (n_chunks + 1)
    n_sel = pre_ref[base + n_chunks]
    n_blocks = (n_sel + (MOE_SLOTS - 1)) >> SLOT_SHIFT

    def chunk_blocks(c):
        s0, s1 = pre_ref[base + c], pre_ref[base + c + 1]
        lo = s0 >> SLOT_SHIFT
        hi = (s1 + (MOE_SLOTS - 1)) >> SLOT_SHIFT
        return lo, jnp.where(s1 > s0, hi, lo)

    @pl.when((e == 0) & (k == 0))
    def _():
        y_ref[...] = jnp.zeros_like(y_ref)

    @pl.when(k == 0)
    def _():
        def clear(q, carry):
            xs_ref[_slot_rows(q), :] = jnp.zeros((MOE_SLOTS, xs_ref.shape[1]), BF16)
            return carry
        lax.fori_loop(0, n_blocks, clear, 0)
        for c in range(n_chunks):
            rank_row = rt_ref[0, :, c * chunk:(c + 1) * chunk]
            xc = xb_ref[c * chunk:(c + 1) * chunk, :]
            lo, hi = chunk_blocks(c)

            def gather(q, carry, rank_row=rank_row, xc=xc):
                slot = q * MOE_SLOTS + lax.broadcasted_iota(jnp.int32, (MOE_SLOTS, chunk), 0)
                onehot = (slot.astype(F32) == rank_row).astype(BF16)
                xs_ref[_slot_rows(q), :] += _dot(onehot, xc).astype(BF16)
                return carry
            lax.fori_loop(lo, hi, gather, 0)

    def ffn(start, size):
        rows = pl.ds(pl.multiple_of(start, MOE_SLOTS), size)
        xq = xs_ref[rows, :]
        g = _dot(xq, wg_ref[0, 0])
        u = _dot(xq, wu_ref[0, 0])
        h = (g * _sigmoid(g) * u).astype(BF16)
        part = _dot(h, wd_ref[0, 0])

        @pl.when(k == 0)
        def _():
            ys_ref[rows, :] = part

        @pl.when(k > 0)
        def _():
            ys_ref[rows, :] += part

    def ffn_quad(p, carry):
        ffn(p * (4 * MOE_SLOTS), 4 * MOE_SLOTS)
        return carry
    n_quads = n_blocks >> 2
    lax.fori_loop(0, n_quads, ffn_quad, 0)

    @pl.when((n_blocks & 2) != 0)
    def _():
        ffn(n_quads * (4 * MOE_SLOTS), 2 * MOE_SLOTS)

    @pl.when((n_blocks & 1) != 0)
    def _():
        ffn((n_blocks - 1) * MOE_SLOTS, MOE_SLOTS)

    @pl.when(k == pl.num_programs(2) - 1)
    def _():
        for c in range(n_chunks):
            tok = slice(c * chunk, (c + 1) * chunk)
            lane = lax.broadcasted_iota(jnp.int32, (chunk, LANES), 1)
            rank_col = jnp.sum(jnp.where(lane == e, r_ref[tok, :], 0.0), axis=1, keepdims=True)
            gate_col = jnp.sum(jnp.where(lane == e, c_ref[tok, :], 0.0), axis=1, keepdims=True)
            lo, hi = chunk_blocks(c)

            def scatter(q, carry, tok=tok, rank_col=rank_col, gate_col=gate_col):
                slot = q * MOE_SLOTS + lax.broadcasted_iota(jnp.int32, (chunk, MOE_SLOTS), 1)
                onehot = (rank_col == slot.astype(F32)).astype(BF16)
                y_ref[tok, :] += gate_col * _dot(onehot, ys_ref[_slot_rows(q), :].astype(BF16))
                return carry
            lax.fori_loop(lo, hi, scatter, 0)


def _moe(xb, combine, rank, w_gu, w_down, a):
    t, d = xb.shape
    n_exp, f = w_down.shape[1], w_down.shape[2]
    tm = min(MOE_TOKEN_TILE, t)
    nt = t // tm
    chunk = min(MOE_CHUNK, tm)
    n_chunks = tm // chunk
    cap = max(-(-tm // MOE_SLOTS), 4) * MOE_SLOTS
    tf = MOE_FF_TILE if f % MOE_FF_TILE == 0 else min(FF_TILE, f)
    nf = f // tf
    member = (rank[:, :n_exp] >= 0).astype(jnp.int32)
    per_chunk = member.reshape(nt, n_chunks, chunk, n_exp).sum(axis=2)
    pre = jnp.concatenate([jnp.zeros((nt, 1, n_exp), jnp.int32), jnp.cumsum(per_chunk, axis=1)], axis=1)
    pre = jnp.swapaxes(pre, 1, 2).reshape(-1)
    rank_t = jnp.swapaxes(rank[:, :n_exp], 0, 1)[:, None, :]
    grid_spec = pltpu.PrefetchScalarGridSpec(
        num_scalar_prefetch=1,
        grid=(nt, n_exp, nf),
        in_specs=[pl.BlockSpec((tm, d), lambda i, e, k, pre: (i, 0)),
                  pl.BlockSpec((1, 1, tm), lambda i, e, k, pre: (e, 0, i)),
                  pl.BlockSpec((tm, LANES), lambda i, e, k, pre: (i, 0)),
                  pl.BlockSpec((tm, LANES), lambda i, e, k, pre: (i, 0)),
                  pl.BlockSpec((1, 1, d, tf), lambda i, e, k, pre: (a, e, 0, k)),
                  pl.BlockSpec((1, 1, d, tf), lambda i, e, k, pre: (a, e, 0, k + nf)),
                  pl.BlockSpec((1, 1, tf, d), lambda i, e, k, pre: (a, e, k, 0))],
        out_specs=pl.BlockSpec((tm, d), lambda i, e, k, pre: (i, 0)),
        scratch_shapes=[pltpu.VMEM((cap, d), BF16), pltpu.VMEM((cap, d), F32)])
    return pl.pallas_call(
        functools.partial(_moe_kernel, n_chunks=n_chunks, chunk=chunk),
        grid_spec=grid_spec,
        out_shape=jax.ShapeDtypeStruct((t, d), F32),
        compiler_params=_params("parallel", "arbitrary", "arbitrary"),
        name="moe_sparse",
    )(pre, xb, rank_t, rank, combine, w_gu, w_gu, w_down)


def _residual_ln_kernel(x_ref, y_ref, lng_ref, lnb_ref, o_ref, *, alpha):
    o_ref[...] = _layer_norm(alpha * x_ref[...] + y_ref[...], lng_ref[0, 0], lnb_ref[0, 0])


def _residual_ln(x, y, ln_g, ln_b, li, alpha):
    t, d = x.shape
    tm = _token_tile(t)
    tok = lambda: pl.BlockSpec((tm, d), lambda i: (i, 0))
    return pl.pallas_call(
        functools.partial(_residual_ln_kernel, alpha=alpha),
        grid=(t // tm,),
        in_specs=[tok(), tok(), _ln_spec(d, li, 1, 1), _ln_spec(d, li, 1, 1)],
        out_specs=tok(),
        out_shape=jax.ShapeDtypeStruct((t, d), F32),
        compiler_params=_params("parallel"),
        name="residual_ln",
    )(x, y, ln_g, ln_b)


def _qkv_kernel(x_ref, w_ref, cos_ref, sin_ref, *refs, dils):
    o_refs, z_ref = refs[:N_GROUPS], refs[N_GROUPS]
    j = pl.program_id(1)
    g = j // 3
    kind = j - 3 * g
    z = _dot(x_ref[...].astype(BF16), w_ref[0])
    tm, width = z.shape

    @pl.when(kind < 2)
    def _():
        cos, sin = cos_ref[...], sin_ref[...]
        lane = lax.broadcasted_iota(jnp.int32, cos.shape, 1)
        half = HEAD_DIM // 2
        first = (lane & (HEAD_DIM - 1)) < half
        scale = jnp.where(kind == 0, HEAD_DIM ** -0.5, 1.0)
        for t in range(width // LANES):
            zt = z[:, t * LANES:(t + 1) * LANES]
            partner = jnp.where(first, pltpu.roll(zt, LANES - half, 1), pltpu.roll(zt, half, 1))
            z_ref[t] = (zt * cos + partner * sin) * scale

    @pl.when(kind == 2)
    def _():
        for t in range(width // LANES):
            z_ref[t] = z[:, t * LANES:(t + 1) * LANES]

    for gi, dil in enumerate(dils):
        @pl.when(g == gi)
        def _(gi=gi, dil=dil):
            for t in range(width // LANES):
                lanes = slice(t * LANES, (t + 1) * LANES)
                if dil == 1:
                    o_refs[gi][0, :, lanes] = z_ref[t]
                else:
                    for c in range(dil):
                        o_refs[gi][c, :, lanes] = z_ref[t, pl.ds(c, tm // dil, stride=dil), :]


def _rope_tables(pos):
    half = HEAD_DIM // 2
    inv = ROPE_THETA ** (-jnp.arange(half, dtype=F32) / half)
    ang = pos.astype(F32)[:, None] * inv[None, :]
    cos, sin = jnp.cos(ang), jnp.sin(ang)
    cos2 = jnp.concatenate([cos, cos], axis=1)
    sin2 = jnp.concatenate([-sin, sin], axis=1)
    return jnp.concatenate([cos2, cos2], axis=1), jnp.concatenate([sin2, sin2], axis=1)


def _qkv(x, w_qkv, pos, a, nseq, dils):
    t, d = x.shape
    seq = t // nseq
    tm = _token_tile(seq)
    tiles = seq // tm
    cos, sin = _rope_tables(pos)

    def out_spec(gi, dil):
        return pl.BlockSpec(
            (None, dil, tm // dil, GROUP_WIDTH),
            lambda i, j: (i // tiles, 0, i % tiles, jnp.clip(j - 3 * gi, 0, 2)))

    def w_col(i, j):
        return (a, 0, (j % 3) * N_GROUPS + j // 3)

    return pl.pallas_call(
        functools.partial(_qkv_kernel, dils=dils),
        grid=(t // tm, 3 * N_GROUPS),
        in_specs=[pl.BlockSpec((tm, d), lambda i, j: (i, 0)),
                  pl.BlockSpec((1, d, GROUP_WIDTH), w_col),
                  pl.BlockSpec((tm, 2 * HEAD_DIM), lambda i, j: (i, 0)),
                  pl.BlockSpec((tm, 2 * HEAD_DIM), lambda i, j: (i, 0))],
        out_specs=[out_spec(gi, dil) for gi, dil in enumerate(dils)],
        out_shape=[jax.ShapeDtypeStruct((nseq, dil, seq // dil, 3 * GROUP_WIDTH), F32)
                   for dil in dils],
        scratch_shapes=[pltpu.VMEM((GROUP_WIDTH // LANES, tm, LANES), F32)],
        compiler_params=_params("parallel", "arbitrary"),
        name="qkv_rope",
    )(x, w_qkv, cos, sin)


def _band_attn_kernel(q_ref, kp_ref, kc_ref, vp_ref, vc_ref, o_ref, l_ref, *, dil):
    n, c = pl.program_id(1), pl.program_id(2)
    tq = kp_ref.shape[0]
    subs = q_ref.shape[0] // tq
    qi = lax.broadcasted_iota(jnp.int32, (tq, 2 * tq), 0) + tq
    ki = lax.broadcasted_iota(jnp.int32, (tq, 2 * tq), 1)
    dist = qi - ki
    band = (dist >= 0) & (dist <= tq)
    q_all = q_ref[...]
    k_all = jnp.concatenate([kp_ref[...], kc_ref[...]], axis=0).astype(BF16)
    v_all = jnp.concatenate([vp_ref[...], vc_ref[...]], axis=0).astype(BF16)
    lower = lax.broadcasted_iota(jnp.int32, (tq, LANES), 1) < HEAD_DIM
    for sub in range(subs):
        mask = band & ((n > 0) | (ki >= tq)) if sub == 0 else band
        q = q_all[sub * tq:(sub + 1) * tq]
        k = k_all[sub * tq:(sub + 2) * tq]
        v = v_all[sub * tq:(sub + 2) * tq]
        outs, lses = [], []
        for t in range(GROUP_WIDTH // LANES):
            sl = slice(t * LANES, (t + 1) * LANES)
            o_t = lse_t = None
            for own in (lower, ~lower):
                qm = jnp.where(own, q[:, sl], 0.0).astype(BF16)
                s = jnp.where(mask, _dot_nt(qm, k[:, sl]), NEG_INF)
                m = jnp.max(s, axis=1, keepdims=True)
                p = jnp.exp(s - m)
                l = jnp.sum(p, axis=1, keepdims=True)
                o_h = _dot(p.astype(BF16), v[:, sl]) / l
                lse_h = jnp.broadcast_to(m + jnp.log(l), (tq, LANES))
                o_t = o_h if o_t is None else jnp.where(lower, o_t, o_h)
                lse_t = lse_h if lse_t is None else jnp.where(lower, lse_t, lse_h)
            outs.append(o_t)
            lses.append(lse_t)
        if dil == 1:
            for t in range(len(outs)):
                o_ref[t, sub * tq:(sub + 1) * tq] = outs[t]
                l_ref[t, sub * tq:(sub + 1) * tq] = lses[t]
        else:
            for cc in range(dil):
                @pl.when(c == cc)
                def _(cc=cc, sub=sub, outs=outs, lses=lses):
                    for t in range(len(outs)):
                        o_ref[t, pl.ds(cc + sub * tq * dil, tq, stride=dil), :] = outs[t]
                        l_ref[t, pl.ds(cc + sub * tq * dil, tq, stride=dil), :] = lses[t]


def _band_attn(qkv_g, g):
    win, dil = GROUPS[g]
    assert win // dil == BAND
    batch, _, lc, _ = qkv_g.shape
    assert lc % BAND == 0
    assert (lc // BAND) % BAND_BLOCKS_PER_STEP == 0
    nblk = lc // (BAND * BAND_BLOCKS_PER_STEP)
    tiles = GROUP_WIDTH // LANES
    prev = lambda n: jnp.maximum(n * BAND_BLOCKS_PER_STEP - 1, 0)
    blk = (None, None, BAND, GROUP_WIDTH)
    cur = (None, None, BAND * BAND_BLOCKS_PER_STEP, GROUP_WIDTH)
    out = pl.BlockSpec((tiles, BAND * BAND_BLOCKS_PER_STEP * dil, LANES),
                       lambda b, n, c: (0, b * nblk + n, 0))
    return pl.pallas_call(
        functools.partial(_band_attn_kernel, dil=dil),
        grid=(batch, nblk, dil),
        in_specs=[pl.BlockSpec(cur, lambda b, n, c: (b, c, n, 0)),
                  pl.BlockSpec(blk, lambda b, n, c: (b, c, prev(n), 1)),
                  pl.BlockSpec(cur, lambda b, n, c: (b, c, n, 1)),
                  pl.BlockSpec(blk, lambda b, n, c: (b, c, prev(n), 2)),
                  pl.BlockSpec(cur, lambda b, n, c: (b, c, n, 2))],
        out_specs=[out, out],
        out_shape=[jax.ShapeDtypeStruct((tiles, batch * lc * dil, LANES), F32)] * 2,
        compiler_params=_params("parallel", "arbitrary", "arbitrary"),
        name=f"band_attn_g{g}",
    )(qkv_g, qkv_g, qkv_g, qkv_g, qkv_g)


def _sample_attn_kernel(q_ref, kn_ref, vn_ref, k_ref, v_ref, *refs, dil, steps):
    ko_ref, vo_ref, o_ref, l_ref = refs[-4:]
    win = k_ref.shape[-1]
    last = win - LANES
    lane_w = lax.broadcasted_iota(jnp.int32, (steps, win), 1)
    lane_f = lax.broadcasted_iota(jnp.int32, (steps, LANES), 1)
    step_w = lax.broadcasted_iota(jnp.int32, (steps, win), 0)
    step_f = lax.broadcasted_iota(jnp.int32, (steps, LANES), 0)
    row_w = lane_w + steps
    mask_w = (((row_w - step_w) & (dil - 1)) == 0) & (row_w <= win + step_w)
    mask_f = (lane_f < steps) & (lane_f >= step_f) & (((lane_f - step_f) & (dil - 1)) == 0)
    new_lane = lax.broadcasted_iota(jnp.int32, (HEAD_DIM, LANES), 1) >= LANES - steps

    def shifted(old, new):
        moved = pltpu.roll(old, win - steps, 1)
        tail = jnp.where(new_lane, new, moved[:, last:])
        return tail if last == 0 else jnp.concatenate([moved[:, :last], tail], axis=1)

    rows = k_ref.shape[0]
    group = rows // kn_ref.shape[0]

    def one_row(i, u):
        r = i * group + u
        to_tail = (LANES - steps - u * steps) % LANES
        place = lambda t: t if to_tail == 0 else pltpu.roll(t, to_tail, 1)
        k_old, v_old = k_ref[r], v_ref[r]
        k_new = shifted(k_old, place(kn_ref[i]))
        v_new = shifted(v_old, place(vn_ref[i]))
        ko_ref[r] = k_new
        vo_ref[r] = v_new
        q = q_ref[r].astype(BF16)
        s_w = jnp.where(mask_w, _dot(q, k_new.astype(BF16)), NEG_INF)
        s_f = jnp.where(mask_f, _dot(q, k_old[:, :LANES].astype(BF16)), NEG_INF)
        m = jnp.maximum(jnp.max(s_w, axis=1, keepdims=True), jnp.max(s_f, axis=1, keepdims=True))
        p_w = jnp.exp(s_w - m)
        p_f = jnp.exp(s_f - m)
        l = jnp.sum(p_w, axis=1, keepdims=True) + jnp.sum(p_f, axis=1, keepdims=True)
        o = (_dot_nt(p_w.astype(BF16), v_new.astype(BF16)) +
             _dot_nt(p_f.astype(BF16), v_old[:, :LANES].astype(BF16)))
        o_ref[r] = o / l
        l_ref[r] = jnp.broadcast_to(m + jnp.log(l), (steps, HEAD_DIM))

    def body(i, carry):
        for u in range(group):
            one_row(i, u)
        return carry
    lax.fori_loop(0, rows // group, body, 0)


def _cache_rows_per_step(rows, win):
    return max(1, min(rows, CACHE_BLOCK_BYTES // (HEAD_DIM * win * 4)))


def _sample_attn(q, k_new_t, v_new_t, cache_kt, cache_vt, prev_out, a, g):
    win, dil = GROUPS[g]
    assert win // dil == BAND
    layers, rows, _, w = cache_kt.shape
    assert w == win
    steps = q.shape[1]
    rb = _cache_rows_per_step(rows, win)
    pack = rows // k_new_t.shape[0]
    assert rb % pack == 0
    cache = lambda: pl.BlockSpec((None, rb, HEAD_DIM, win), lambda i: (a, i, 0, 0))
    tok = lambda: pl.BlockSpec((rb, steps, HEAD_DIM), lambda i: (i, 0, 0))
    new = lambda: pl.BlockSpec((rb // pack, HEAD_DIM, LANES), lambda i: (i, 0, 0))
    in_specs = [tok(), new(), new(), cache(), cache()]
    args = [q, k_new_t, v_new_t, cache_kt, cache_vt]
    aliases = {}
    if prev_out is not None:
        in_specs += [pl.BlockSpec(memory_space=pl.ANY)] * 2
        args += list(prev_out)
        aliases = {5: 0, 6: 1}
    return pl.pallas_call(
        functools.partial(_sample_attn_kernel, dil=dil, steps=steps),
        grid=(rows // rb,),
        in_specs=in_specs,
        out_specs=[cache(), cache(), tok(), tok()],
        out_shape=[jax.ShapeDtypeStruct(cache_kt.shape, F32)] * 2 +
                  [jax.ShapeDtypeStruct(q.shape, F32)] * 2,
        input_output_aliases=aliases,
        compiler_params=_params("parallel"),
        name=f"sample_attn_g{g}",
    )(*args)


def _merge_kernel(*refs, alpha):
    o_refs = refs[0:N_GROUPS]
    l_refs = refs[N_GROUPS:2 * N_GROUPS]
    x_ref, wo_ref, lng_ref, lnb_ref, out_ref, outb_ref = refs[2 * N_GROUPS:]
    lses = [r[...] for r in l_refs]
    top = functools.reduce(jnp.maximum, lses)
    es = [jnp.exp(l - top) for l in lses]
    den = functools.reduce(jnp.add, es)
    y = functools.reduce(jnp.add, [(e / den) * r[...] for e, r in zip(es, o_refs)])
    y = jnp.concatenate([y[t] for t in range(y.shape[0])], axis=1)
    m = _dot(y.astype(BF16), wo_ref[0])
    out = _layer_norm(alpha * x_ref[...] + m, lng_ref[0, 0], lnb_ref[0, 0])
    out_ref[...] = out
    outb_ref[...] = out.astype(BF16)


def _merge(outs, lses, x, w_o, ln_g, ln_b, a, li, alpha):
    t, d = x.shape
    tm = min(512, t)
    grp = lambda: pl.BlockSpec((GROUP_WIDTH // LANES, tm, LANES), lambda i: (0, i, 0))
    tok = lambda: pl.BlockSpec((tm, d), lambda i: (i, 0))
    return pl.pallas_call(
        functools.partial(_merge_kernel, alpha=alpha),
        grid=(t // tm,),
        in_specs=[grp() for _ in range(2 * N_GROUPS)] + [
            tok(), pl.BlockSpec((1, GROUP_WIDTH, d), lambda i: (a, 0, 0)),
            _ln_spec(d, li, 0, 1), _ln_spec(d, li, 0, 1)],
        out_specs=[tok(), tok()],
        out_shape=[jax.ShapeDtypeStruct((t, d), F32), jax.ShapeDtypeStruct((t, d), BF16)],
        compiler_params=_params("parallel"),
        name="merge_wo",
    )(*outs, *lses, x, w_o, ln_g, ln_b)


def _heads_first(x, nbatch, steps):
    x = x.reshape(steps, nbatch, HEADS, HEAD_DIM)
    return jnp.transpose(x, (1, 2, 0, 3)).reshape(nbatch * HEADS, steps, HEAD_DIM)


def _new_columns(x, nbatch, steps, pack):
    x = x.reshape(steps, nbatch * HEADS // pack, pack, HEAD_DIM)
    x = jnp.transpose(x, (1, 3, 2, 0)).reshape(nbatch * HEADS // pack, HEAD_DIM, pack * steps)
    return jnp.pad(x, ((0, 0), (0, 0), (0, LANES - pack * steps)))


def _tokens_first(x, nbatch, steps):
    x = x.reshape(nbatch, HEADS // 2, 2, steps, HEAD_DIM)
    return jnp.transpose(x, (1, 3, 0, 2, 4)).reshape(HEADS // 2, steps * nbatch, LANES)


def kernel(x_prompt, x_sample, state_conv, cache_k_w128, cache_v_w128, cache_k_w512, cache_v_w512, cache_k_w2048, cache_v_w2048, ln_g, ln_b, w_pw1, b_pw1, w_dw, b_dw, cn_g, cn_b, w_pw2, b_pw2, w_qkv, w_o, w_ffn_gu, w_ffn_down, w_router, b_router, w_exp_gu, w_exp_down):
    batch, seq, d = x_prompt.shape
    nbatch, s_new, _ = x_sample.shape
    depth = ln_g.shape[0]
    alpha = (2 * depth) ** 0.25
    caches = ((cache_k_w128, cache_v_w128), (cache_k_w512, cache_v_w512),
              (cache_k_w2048, cache_v_w2048))
    caches_t = [[jnp.transpose(c, (0, 1, 3, 4, 2)).reshape(c.shape[0], nbatch * HEADS, HEAD_DIM, c.shape[2])
                 for c in kv] for kv in caches]

    w_pw1_h, w_pw2_h, w_qkv_h, w_o_h = (w.astype(BF16) for w in (w_pw1, w_pw2, w_qkv, w_o))
    w_ffn_gu_h, w_ffn_down_h = w_ffn_gu.astype(BF16), w_ffn_down.astype(BF16)
    w_exp_gu_h, w_exp_down_h = w_exp_gu.astype(BF16), w_exp_down.astype(BF16)
    row = lambda v: v[:, None, :]
    conv_wts = (w_dw, row(b_dw), row(cn_g), row(cn_b), w_pw2_h, row(b_pw2))
    ln_g4, ln_b4 = ln_g[:, :, None, :], ln_b[:, :, None, :]
    w_router_t = jnp.swapaxes(w_router, 1, 2)
    b_router3 = row(b_router)
    state_t = jnp.transpose(state_conv, (0, 2, 1, 3))

    pos_p = jnp.tile(jnp.arange(seq), batch)
    pos_s = jnp.repeat(PAST_LEN + jnp.arange(s_new), nbatch)
    dils = tuple(dil for _, dil in GROUPS)

    xp = x_prompt.reshape(batch * seq, d)
    xs = jnp.transpose(x_sample, (1, 0, 2)).reshape(s_new * nbatch, d)
    conv_p, conv_s = [], []
    kv_p = [[] for _ in range(2 * N_GROUPS)]
    kv_s = [None] * N_GROUPS
    for li in range(depth):
        a = li // 2
        if li % 2 == 0:
            wts = conv_wts + (ln_g4, ln_b4)
            up = _pw1_glu(xp, w_pw1_h, row(b_pw1), a)
            us = _pw1_glu(xs, w_pw1_h, row(b_pw1), a)
            xp = _conv_prompt(up, xp, batch, wts, a, li, alpha)
            xs, st = _conv_sample(us, xs, state_t, wts, a, li, alpha)
            conv_p.append(up.reshape(batch, seq, d)[:, seq - CONV_HIST:])
            conv_s.append(st)
            xp = _ffn(xp, w_ffn_gu_h, w_ffn_down_h, ln_g4, ln_b4, a, li, alpha)
            xs = _ffn(xs, w_ffn_gu_h, w_ffn_down_h, ln_g4, ln_b4, a, li, alpha)
        else:
            qkv_p = _qkv(xp, w_qkv_h, pos_p, a, batch, dils)
            qkv_s = _qkv(xs, w_qkv_h, pos_s, a, 1, (1,) * N_GROUPS)
            outs_p, lses_p, outs_s, lses_s = [], [], [], []
            for g, (win, dil) in enumerate(GROUPS):
                o, l = _band_attn(qkv_p[g], g)
                outs_p.append(o)
                lses_p.append(l)
                lc = seq // dil
                for j in range(2):
                    tail = qkv_p[g][:, :, lc - BAND:, (1 + j) * GROUP_WIDTH:(2 + j) * GROUP_WIDTH]
                    tail = jnp.swapaxes(tail, 1, 2).reshape(batch, win, HEADS, HEAD_DIM)
                    kv_p[2 * g + j].append(tail)
                part = lambda j: qkv_s[g][0, 0, :, j * GROUP_WIDTH:(j + 1) * GROUP_WIDTH]
                pack = min(_cache_rows_per_step(nbatch * HEADS, win), LANES // s_new)
                res = _sample_attn(_heads_first(part(0), nbatch, s_new),
                                   _new_columns(part(1), nbatch, s_new, pack),
                                   _new_columns(part(2), nbatch, s_new, pack),
                                   caches_t[g][0], caches_t[g][1], kv_s[g], a, g)
                kv_s[g] = res[:2]
                outs_s.append(_tokens_first(res[2], nbatch, s_new))
                lses_s.append(_tokens_first(res[3], nbatch, s_new))
            xp, xpb = _merge(outs_p, lses_p, xp, w_o_h, ln_g4, ln_b4, a, li, alpha)
            xs, xsb = _merge(outs_s, lses_s, xs, w_o_h, ln_g4, ln_b4, a, li, alpha)
            for stream in range(2):
                x, xb = (xp, xpb) if stream == 0 else (xs, xsb)
                combine, rank = _router(x, w_router_t, b_router3, a, min(MOE_TOKEN_TILE, x.shape[0]))
                y = _moe(xb, combine, rank, w_exp_gu_h, w_exp_down_h, a)
                x = _residual_ln(x, y, ln_g4, ln_b4, li, alpha)
                if stream == 0:
                    xp = x
                else:
                    xs = x

    kv_p_out = [jnp.stack(t) for t in kv_p]
    kv_s_out = []
    for g, (win, _) in enumerate(GROUPS):
        for j in range(2):
            ct = kv_s[g][j].reshape(-1, nbatch, HEADS, HEAD_DIM, win)
            kv_s_out.append(jnp.transpose(ct, (0, 1, 4, 2, 3)))
    conv_s_out = jnp.transpose(jnp.concatenate(conv_s, axis=0), (0, 2, 1, 3))
    y_sample = jnp.transpose(xs.reshape(s_new, nbatch, d), (1, 0, 2))
    return (xp.reshape(batch, seq, d), y_sample, jnp.stack(conv_p), conv_s_out,
            *kv_p_out, *kv_s_out)
```
